```python
import math
import jax, jax.numpy as jnp
from jax import lax
import numpy as np

D_MODEL = 2048
BATCH = 4
SEQ = 4096
DEPTH = 4

D_FF = 5632
D_HYENA = D_MODEL // 2
HYENA_ORDER = 2
SHORT_CONV = 3
FILTER_BANDS = 16
FILTER_EMB = 1 + 2 * FILTER_BANDS
FILTER_HIDDEN = 64
FILTER_INNER = 2
DECAY_FAST_PCT = 0.3
DECAY_SLOW_PCT = 1.5
DECAY_TARGET = 1e-2
N_DIRS = 2
D_SGU = D_MODEL // 2
SGU_CHUNK = 128
SGU_GROUPS = 8
N_BRANCH = 2
D_PROJ = (HYENA_ORDER + 1) * D_HYENA + 2 * D_SGU + N_BRANCH * D_MODEL
NORM_EPS = 1e-6
LN_EPS = 1e-5

kernel_name = "hybrid_hyena_gmlp_macaron_encoder"


def rms_norm(x, g):
    xf = x.astype(jnp.float32)
    y = xf * lax.rsqrt(jnp.mean(xf * xf, axis=-1, keepdims=True) + NORM_EPS)
    return (y * g.astype(jnp.float32)).astype(x.dtype)


def layer_norm(x, g, b):
    xf = x.astype(jnp.float32)
    mu = jnp.mean(xf, axis=-1, keepdims=True)
    var = jnp.mean(jnp.square(xf - mu), axis=-1, keepdims=True)
    y = (xf - mu) * lax.rsqrt(var + LN_EPS)
    return (y * g.astype(jnp.float32) + b.astype(jnp.float32)).astype(x.dtype)


def half_step_ffn(x, pre_g, w_gate, w_up, w_down, post_g):
    h = rms_norm(x, pre_g)
    h = (jax.nn.silu(h @ w_gate) * (h @ w_up)) @ w_down
    return x + 0.5 * rms_norm(h, post_g)


def centred_short_conv(u, w, b):
    L = u.shape[1]
    pad = SHORT_CONV // 2
    up = jnp.pad(u, ((0, 0), (pad, pad), (0, 0)))
    y = b
    for j in range(SHORT_CONV):
        y = y + up[:, j:j + L] * w[j]
    return y


def filter_positions(L):
    t = jnp.linspace(0.0, 1.0, L, dtype=jnp.float32)[:, None]
    w = (2.0 * math.pi / L) * jnp.arange(L, dtype=jnp.float32)[:, None]
    f = jnp.linspace(1e-4, FILTER_BANDS - 1, FILTER_BANDS, dtype=jnp.float32)[None, :]
    z = jnp.concatenate([t, jnp.cos(f * w), -jnp.sin(f * w)], axis=-1)
    return t, z


def implicit_filter_spectra(t, z, w1, b, freq, w_inner, w_out):
    L = z.shape[0]
    f32 = jnp.float32
    b = b.astype(f32)
    freq = freq.astype(f32)
    h = jnp.sin(freq[0] * (z @ w1.astype(f32) + b[0]))
    for j in range(FILTER_INNER):
        h = jnp.sin(freq[j + 1] * (h @ w_inner[j].astype(f32) + b[j + 1]))
    h = h @ w_out.astype(f32)
    max_decay = math.log(DECAY_TARGET) / DECAY_FAST_PCT
    min_decay = math.log(DECAY_TARGET) / DECAY_SLOW_PCT
    deltas = jnp.linspace(min_decay, max_decay, D_HYENA, dtype=f32)
    decay = jnp.exp(-t * jnp.abs(deltas)[None, :])
    h = h.reshape(L, HYENA_ORDER, N_DIRS, D_HYENA) * decay[:, None, None, :]
    h_fwd = h[:, :, 0]
    h_bwd = h[:, :, 1]
    k = jnp.concatenate([h_fwd, jnp.zeros_like(h_fwd[:1]), h_bwd[:0:-1]], axis=0)
    return jnp.fft.rfft(k, axis=0)


def long_conv(u, k_f, skip):
    L = u.shape[1]
    uf32 = u.astype(jnp.float32)
    uf = jnp.fft.rfft(uf32, n=2 * L, axis=1)
    y = jnp.fft.irfft(uf * k_f[None], n=2 * L, axis=1)[:, :L]
    return (y + uf32 * skip.astype(jnp.float32)).astype(u.dtype)


def hyena_mixer(proj, conv_w, conv_b, t, z, f_w1, f_b, f_freq, f_w_inner, f_w_out, skip):
    u = centred_short_conv(proj, conv_w, conv_b)
    parts = jnp.split(u, HYENA_ORDER + 1, axis=-1)
    y = parts[0]
    k_f = implicit_filter_spectra(t, z, f_w1, f_b, f_freq, f_w_inner, f_w_out)
    for o in range(HYENA_ORDER):
        y = parts[o + 1] * long_conv(y, k_f[:, o], skip[o])
    return y


def spatial_gating(proj, ln_g, ln_b, w_s, b_s):
    zz = jax.nn.gelu(proj)
    u, v = jnp.split(zz, 2, axis=-1)
    v = layer_norm(v, ln_g, ln_b)
    B, L, C = v.shape
    v = v.reshape(B, L // SGU_CHUNK, SGU_CHUNK, SGU_GROUPS, C // SGU_GROUPS)
    s = jnp.einsum('gts,bcsgd->bctgd', w_s, v) + b_s.T[None, None, :, :, None]
    return u * s.reshape(B, L, C)


def setup_inputs(seed: int = 0) -> dict:
    key = jax.random.key(seed)
    keys = iter(jax.random.split(key, 40))
    f32 = jnp.float32

    def nrm(shape, scale):
        return scale * jax.random.normal(next(keys), shape, f32)

    def gain(shape):
        return 1.0 + nrm(shape, 0.05)

    D, F, DH, DG = D_MODEL, D_FF, D_HYENA, D_SGU
    inp = {}
    inp["x"] = nrm((BATCH, SEQ, D), 1.0)
    inp["ffn1_pre_g"] = gain((DEPTH, D))
    inp["ffn1_w_gate"] = nrm((DEPTH, D, F), D ** -0.5)
    inp["ffn1_w_up"] = nrm((DEPTH, D, F), D ** -0.5)
    inp["ffn1_w_down"] = nrm((DEPTH, F, D), F ** -0.5)
    inp["ffn1_post_g"] = gain((DEPTH, D))
    inp["mix_pre_g"] = gain((DEPTH, D))
    inp["w_in"] = nrm((DEPTH, D, D_PROJ), D ** -0.5)
    inp["hy_conv_w"] = nrm((DEPTH, SHORT_CONV, (HYENA_ORDER + 1) * DH), SHORT_CONV ** -0.5)
    inp["hy_conv_b"] = nrm((DEPTH, (HYENA_ORDER + 1) * DH), 0.02)
    inp["filt_w1"] = nrm((DEPTH, FILTER_EMB, FILTER_HIDDEN), FILTER_EMB ** -0.5)
    inp["filt_b"] = nrm((DEPTH, FILTER_INNER + 1, FILTER_HIDDEN), 0.1)
    inp["filt_freq"] = gain((DEPTH, FILTER_INNER + 1, FILTER_HIDDEN))
    inp["filt_w_inner"] = nrm((DEPTH, FILTER_INNER, FILTER_HIDDEN, FILTER_HIDDEN), FILTER_HIDDEN ** -0.5)
    inp["filt_w_out"] = nrm((DEPTH, FILTER_HIDDEN, HYENA_ORDER * N_DIRS * DH), 0.02 * FILTER_HIDDEN ** -0.5)
    inp["hy_skip"] = nrm((DEPTH, HYENA_ORDER, DH), 1.0)
    inp["sgu_ln_g"] = gain((DEPTH, DG))
    inp["sgu_ln_b"] = nrm((DEPTH, DG), 0.02)
    inp["sgu_w_s"] = nrm((DEPTH, SGU_GROUPS, SGU_CHUNK, SGU_CHUNK), SGU_CHUNK ** -0.5)
    inp["sgu_b_s"] = gain((DEPTH, SGU_GROUPS, SGU_CHUNK))
    inp["p_a"] = nrm((DEPTH, DH, D), DH ** -0.5)
    inp["p_b"] = nrm((DEPTH, DG, D), DG ** -0.5)
    inp["w_out"] = nrm((DEPTH, D, D), D ** -0.5)
    inp["mix_post_g"] = gain((DEPTH, D))
    inp["ffn2_pre_g"] = gain((DEPTH, D))
    inp["ffn2_w_gate"] = nrm((DEPTH, D, F), D ** -0.5)
    inp["ffn2_w_up"] = nrm((DEPTH, D, F), D ** -0.5)
    inp["ffn2_w_down"] = nrm((DEPTH, F, D), F ** -0.5)
    inp["ffn2_post_g"] = gain((DEPTH, D))
    return inp


def reference(x, ffn1_pre_g, ffn1_w_gate, ffn1_w_up, ffn1_w_down, ffn1_post_g,
              mix_pre_g, w_in, hy_conv_w, hy_conv_b, filt_w1, filt_b, filt_freq,
              filt_w_inner, filt_w_out, hy_skip, sgu_ln_g, sgu_ln_b, sgu_w_s, sgu_b_s,
              p_a, p_b, w_out, mix_post_g,
              ffn2_pre_g, ffn2_w_gate, ffn2_w_up, ffn2_w_down, ffn2_post_g):
    L = x.shape[1]
    t, z = filter_positions(L)
    split_at = [(HYENA_ORDER + 1) * D_HYENA, (HYENA_ORDER + 1) * D_HYENA + 2 * D_SGU]
    for i in range(DEPTH):
        x = half_step_ffn(x, ffn1_pre_g[i], ffn1_w_gate[i], ffn1_w_up[i], ffn1_w_down[i], ffn1_post_g[i])
        n = rms_norm(x, mix_pre_g[i])
        proj = n @ w_in[i]
        hy_in, sg_in, gates = jnp.split(proj, split_at, axis=-1)
        a = hyena_mixer(hy_in, hy_conv_w[i], hy_conv_b[i], t, z, filt_w1[i], filt_b[i],
                        filt_freq[i], filt_w_inner[i], filt_w_out[i], hy_skip[i])
        b = spatial_gating(sg_in, sgu_ln_g[i], sgu_ln_b[i], sgu_w_s[i], sgu_b_s[i])
        g_a, g_b = jnp.split(gates, N_BRANCH, axis=-1)
        merged = jax.nn.sigmoid(g_a) * (a @ p_a[i]) + jax.nn.sigmoid(g_b) * (b @ p_b[i])
        x = x + rms_norm(merged @ w_out[i], mix_post_g[i])
        x = half_step_ffn(x, ffn2_pre_g[i], ffn2_w_gate[i], ffn2_w_up[i], ffn2_w_down[i], ffn2_post_g[i])
    return x
```

```python
import functools
import math

import jax
import jax.numpy as jnp
from jax import lax
from jax.experimental import pallas as pl
from jax.experimental.pallas import tpu as pltpu

F32 = jnp.float32
BF16 = jnp.bfloat16

D_MODEL = 2048
D_FF = 5632
D_HYENA = D_MODEL // 2
HYENA_ORDER = 2
SHORT_CONV = 3
FILTER_BANDS = 16
FILTER_EMB = 1 + 2 * FILTER_BANDS
FILTER_HIDDEN = 64
FILTER_INNER = 2
DECAY_FAST_PCT = 0.3
DECAY_SLOW_PCT = 1.5
DECAY_TARGET = 1e-2
N_DIRS = 2
D_SGU = D_MODEL // 2
SGU_CHUNK = 128
SGU_GROUPS = 8
NORM_EPS = 1e-6
LN_EPS = 1e-5

LANES = 128
SUBLANES = 8
VMEM_LIMIT_BYTES = 56 * 1024 * 1024

FFN_TM = 512
FFN_TF = 512
PROJ_TM = 512
MIX_TM = 256
CONV_TC = 256
FILT_TM = 512
DFT_TM = 512
FSPEC_TM = 256


def _params(*sem):
    return pltpu.CompilerParams(dimension_semantics=sem, vmem_limit_bytes=VMEM_LIMIT_BYTES)


def _resident(shape, index_map):
    return pl.BlockSpec(shape, index_map, pipeline_mode=pl.Buffered(1))


def _rms(x):
    return x * lax.rsqrt(jnp.mean(x * x, axis=-1, keepdims=True) + NORM_EPS)


def _dot(a, b):
    return jnp.dot(a, b, preferred_element_type=F32)


def _ffn_kernel(x_ref, pre_g_ref, wg_ref, wu_ref, wd_ref, post_g_ref, o_ref, xn_ref, acc_ref):
    j = pl.program_id(1)

    @pl.when(j == 0)
    def _():
        xn_ref[...] = (_rms(x_ref[...]) * pre_g_ref[...]).astype(BF16)

    xn = xn_ref[...]
    g = _dot(xn, wg_ref[...])
    u = _dot(xn, wu_ref[...])
    h = (g * jax.nn.sigmoid(g) * u).astype(BF16)
    d = _dot(h, wd_ref[...])

    @pl.when(j == 0)
    def _():
        acc_ref[...] = d

    @pl.when(j > 0)
    def _():
        acc_ref[...] += d

    @pl.when(j == pl.num_programs(1) - 1)
    def _():
        o_ref[...] = x_ref[...] + 0.5 * (_rms(acc_ref[...]) * post_g_ref[...])


def _ffn(x, pre_g, wg, wu, wd, post_g):
    n, d = x.shape
    f = wg.shape[1]
    tm, tf = FFN_TM, FFN_TF
    return pl.pallas_call(
        _ffn_kernel,
        grid=(n // tm, f // tf),
        in_specs=[
            pl.BlockSpec((tm, d), lambda i, j: (i, 0)),
            pl.BlockSpec((1, d), lambda i, j: (0, 0)),
            pl.BlockSpec((d, tf), lambda i, j: (0, j)),
            pl.BlockSpec((d, tf), lambda i, j: (0, j)),
            pl.BlockSpec((tf, d), lambda i, j: (j, 0)),
            pl.BlockSpec((1, d), lambda i, j: (0, 0)),
        ],
        out_specs=pl.BlockSpec((tm, d), lambda i, j: (i, 0)),
        out_shape=jax.ShapeDtypeStruct((n, d), F32),
        scratch_shapes=[pltpu.VMEM((tm, d), BF16), pltpu.VMEM((tm, d), F32)],
        compiler_params=_params("parallel", "arbitrary"),
        name="ffn",
    )(x, pre_g, wg, wu, wd, post_g)


def _proj_kernel(x_ref, g_ref, w_ref, o_ref):
    xn = (_rms(x_ref[...]) * g_ref[...]).astype(BF16)
    o_ref[...] = _dot(xn, w_ref[...])


def _proj(x, g, w):
    n, d = x.shape
    c = w.shape[1]
    tm = PROJ_TM
    return pl.pallas_call(
        _proj_kernel,
        grid=(n // tm,),
        in_specs=[
            pl.BlockSpec((tm, d), lambda i: (i, 0)),
            pl.BlockSpec((1, d), lambda i: (0, 0)),
            _resident((d, c), lambda i: (0, 0)),
        ],
        out_specs=pl.BlockSpec((tm, c), lambda i: (i, 0)),
        out_shape=jax.ShapeDtypeStruct((n, c), F32),
        compiler_params=_params("parallel"),
        name="hyena_proj",
    )(x, g, w)


def _short_conv_kernel(p_ref, w_ref, b_ref, o_ref):
    p = p_ref[0]
    seq = p.shape[0]
    row = lax.broadcasted_iota(jnp.int32, p.shape, 0)
    prev = jnp.where(row == 0, 0.0, pltpu.roll(p, 1, 0))
    nxt = jnp.where(row == seq - 1, 0.0, pltpu.roll(p, seq - 1, 0))
    w = w_ref[...]
    y = b_ref[...] + prev * w[0:1] + p * w[1:2] + nxt * w[2:3]
    o_ref[0] = y.astype(o_ref.dtype)


def _short_conv(p, w, b):
    bsz, seq, c = p.shape
    tc = CONV_TC
    return pl.pallas_call(
        _short_conv_kernel,
        grid=(bsz, c // tc),
        in_specs=[
            pl.BlockSpec((1, seq, tc), lambda i, j: (i, 0, j)),
            pl.BlockSpec((SHORT_CONV, tc), lambda i, j: (0, j)),
            pl.BlockSpec((1, tc), lambda i, j: (0, j)),
        ],
        out_specs=pl.BlockSpec((1, seq, tc), lambda i, j: (i, 0, j)),
        out_shape=jax.ShapeDtypeStruct((bsz, seq, c), BF16),
        compiler_params=_params("parallel", "parallel"),
        name="short_conv",
    )(p, w, b)


def _filter_kernel(z_ref, w1_ref, b_ref, freq_ref, wi_ref, wo_ref, delta_ref, o_ref):
    hp = functools.partial(jnp.dot, preferred_element_type=F32, precision=lax.Precision.HIGHEST)
    z = z_ref[...]
    b = b_ref[...]
    freq = freq_ref[...]
    h = jnp.sin(freq[0:1] * (hp(z, w1_ref[...]) + b[0:1]))
    for j in range(FILTER_INNER):
        h = jnp.sin(freq[j + 1:j + 2] * (hp(h, wi_ref[j]) + b[j + 1:j + 2]))
    h = hp(h, wo_ref[...])
    t = z[:, 0:1]
    decay = jnp.exp(-t * delta_ref[...])
    tm = z.shape[0]
    row = pl.program_id(0) * tm + lax.broadcasted_iota(jnp.int32, (tm, D_HYENA), 0)
    for blk in range(HYENA_ORDER * N_DIRS):
        hb = h[:, blk * D_HYENA:(blk + 1) * D_HYENA] * decay
        if blk % N_DIRS == 1:
            hb = jnp.where(row == 0, 0.0, hb)
        o_ref[:, blk * D_HYENA:(blk + 1) * D_HYENA] = hb.astype(o_ref.dtype)


def _filter_taps(z, w1, b, freq, w_inner, w_out, abs_delta):
    seq = z.shape[0]
    c = w_out.shape[1]
    tm = FILT_TM
    full = lambda a: pl.BlockSpec(a.shape, lambda i: (0,) * a.ndim)
    return pl.pallas_call(
        _filter_kernel,
        grid=(seq // tm,),
        in_specs=[pl.BlockSpec((tm, z.shape[1]), lambda i: (i, 0)),
                  full(w1), full(b), full(freq), full(w_inner), full(w_out), full(abs_delta)],
        out_specs=pl.BlockSpec((tm, c), lambda i: (i, 0)),
        out_shape=jax.ShapeDtypeStruct((seq, c), BF16),
        compiler_params=_params("parallel"),
        name="filter_taps",
    )(z, w1, b, freq, w_inner, w_out, abs_delta)


def _filter_spectrum_kernel(c_ref, s_ref, h_ref, kr_ref, ki_ref):
    h = h_ref[...]
    a = _dot(c_ref[...], h)
    b = _dot(s_ref[...], h)
    scale = 1.0 / h.shape[0]
    af, ab = a[:, :D_HYENA], a[:, D_HYENA:]
    bf, bb = b[:, :D_HYENA], b[:, D_HYENA:]
    kr_ref[0] = (af + ab) * scale
    ki_ref[0] = (bb - bf) * scale

    @pl.when(pl.program_id(1) == 0)
    def _():
        first = lax.broadcasted_iota(jnp.int32, (SUBLANES, D_HYENA), 0) == 0
        kr8 = (af[:SUBLANES] + ab[:SUBLANES]) * scale
        kr_ref[0, :SUBLANES, :] = jnp.where(first, 0.5 * kr8, kr8)
        ki_ref[0, :SUBLANES, :] = jnp.where(
            first, 0.5 * scale * (bf[:SUBLANES] + bb[:SUBLANES]), (bb[:SUBLANES] - bf[:SUBLANES]) * scale)


def _filter_spectrum(cm, sf, taps):
    seq = cm.shape[0]
    tm = FSPEC_TM
    out = jax.ShapeDtypeStruct((HYENA_ORDER, seq, D_HYENA), F32)
    return pl.pallas_call(
        _filter_spectrum_kernel,
        grid=(HYENA_ORDER, seq // tm),
        in_specs=[
            pl.BlockSpec((tm, seq), lambda o, i: (i, 0)),
            pl.BlockSpec((tm, seq), lambda o, i: (i, 0)),
            _resident((seq, N_DIRS * D_HYENA), lambda o, i: (0, o)),
        ],
        out_specs=[pl.BlockSpec((1, tm, D_HYENA), lambda o, i: (o, i, 0))] * 2,
        out_shape=[out, out],
        compiler_params=_params("parallel", "arbitrary"),
        name="filter_spectrum",
    )(cm, sf, taps)


def _spectral_kernel(c_ref, s_ref, y_ref, kr_ref, ki_ref, z_ref):
    y = y_ref[0]
    a = _dot(c_ref[...], y)
    b = _dot(s_ref[...], y)
    kr = kr_ref[0]
    ki = ki_ref[0]
    z_ref[0, 0] = (a * kr + b * ki).astype(z_ref.dtype)
    z_ref[0, 1] = (b * kr - a * ki).astype(z_ref.dtype)

    @pl.when(pl.program_id(1) == 0)
    def _():
        first = lax.broadcasted_iota(jnp.int32, (SUBLANES, D_HYENA), 0) == 0
        a8, b8, kr8, ki8 = a[:SUBLANES], b[:SUBLANES], kr[:SUBLANES], ki[:SUBLANES]
        z_ref[0, 0, :SUBLANES, :] = jnp.where(first, a8 * kr8, a8 * kr8 + b8 * ki8).astype(z_ref.dtype)
        z_ref[0, 1, :SUBLANES, :] = jnp.where(first, b8 * ki8, b8 * kr8 - a8 * ki8).astype(z_ref.dtype)


def _spectral(cm, sf, y, col_block, kr, ki, order):
    bsz, seq, _ = y.shape
    tm = DFT_TM
    return pl.pallas_call(
        _spectral_kernel,
        grid=(bsz, seq // tm),
        in_specs=[
            pl.BlockSpec((tm, seq), lambda b, i: (i, 0)),
            pl.BlockSpec((tm, seq), lambda b, i: (i, 0)),
            _resident((1, seq, D_HYENA), lambda b, i: (b, 0, col_block)),
            pl.BlockSpec((1, tm, D_HYENA), lambda b, i: (order, i, 0)),
            pl.BlockSpec((1, tm, D_HYENA), lambda b, i: (order, i, 0)),
        ],
        out_specs=pl.BlockSpec((1, 2, tm, D_HYENA), lambda b, i: (b, 0, i, 0)),
        out_shape=jax.ShapeDtypeStruct((bsz, 2, seq, D_HYENA), BF16),
        compiler_params=_params("parallel", "arbitrary"),
        name="hyena_spectral",
    )(cm, sf, y, kr, ki)


def _inverse_gate_kernel(m_ref, z_ref, gate_ref, y_ref, skip_ref, o_ref):
    conv = _dot(m_ref[...], z_ref[0])
    y = y_ref[0].astype(F32)
    o_ref[0] = (gate_ref[0].astype(F32) * (conv + y * skip_ref[0])).astype(o_ref.dtype)


def _inverse_gate(minv, z, gate, gate_block, y, y_block, skip, order):
    bsz, two_seq, _ = z.shape
    seq = two_seq // 2
    tm = DFT_TM
    return pl.pallas_call(
        _inverse_gate_kernel,
        grid=(bsz, seq // tm),
        in_specs=[
            pl.BlockSpec((tm, two_seq), lambda b, i: (i, 0)),
            _resident((1, two_seq, D_HYENA), lambda b, i: (b, 0, 0)),
            pl.BlockSpec((1, tm, D_HYENA), lambda b, i: (b, i, gate_block)),
            pl.BlockSpec((1, tm, D_HYENA), lambda b, i: (b, i, y_block)),
            pl.BlockSpec((1, 1, D_HYENA), lambda b, i: (order, 0, 0)),
        ],
        out_specs=pl.BlockSpec((1, tm, D_HYENA), lambda b, i: (b, i, 0)),
        out_shape=jax.ShapeDtypeStruct((bsz, seq, D_HYENA), BF16),
        compiler_params=_params("parallel", "arbitrary"),
        name="hyena_inverse_gate",
    )(minv, z, gate, y, skip)


def _sgu_kernel(x_ref, g_ref, wsg_ref, wgate_ref, lng_ref, lnb_ref, ws_ref, bs_ref, pb_ref,
                o_ref, s_ref):
    xn = (_rms(x_ref[...]) * g_ref[...]).astype(BF16)
    zz = jax.nn.gelu(_dot(xn, wsg_ref[...]), approximate=True)
    u = zz[:, :D_SGU]
    v = zz[:, D_SGU:]
    mu = jnp.mean(v, axis=-1, keepdims=True)
    vc = v - mu
    var = jnp.mean(vc * vc, axis=-1, keepdims=True)
    v = (vc * lax.rsqrt(var + LN_EPS) * lng_ref[...] + lnb_ref[...]).astype(BF16)
    dg = D_SGU // SGU_GROUPS
    for c in range(x_ref.shape[0] // SGU_CHUNK):
        rows = slice(c * SGU_CHUNK, (c + 1) * SGU_CHUNK)
        for grp in range(SGU_GROUPS):
            cols = slice(grp * dg, (grp + 1) * dg)
            s_ref[rows, cols] = _dot(ws_ref[grp], v[rows, cols]) + bs_ref[grp]
    gated = (u * s_ref[...]).astype(BF16)
    branch = _dot(gated, pb_ref[...])
    gate = jax.nn.sigmoid(_dot(xn, wgate_ref[...]))
    o_ref[...] = (gate * branch).astype(o_ref.dtype)


def _sgu(x, g, w_sg, w_gate, ln_g, ln_b, w_s, b_s, p_b):
    n, d = x.shape
    tm = MIX_TM
    const2 = lambda i: (0, 0)
    const3 = lambda i: (0, 0, 0)
    return pl.pallas_call(
        _sgu_kernel,
        grid=(n // tm,),
        in_specs=[
            pl.BlockSpec((tm, d), lambda i: (i, 0)),
            pl.BlockSpec((1, d), const2),
            _resident(w_sg.shape, const2),
            _resident(w_gate.shape, const2),
            pl.BlockSpec(ln_g.shape, const2),
            pl.BlockSpec(ln_b.shape, const2),
            _resident(w_s.shape, const3),
            _resident(b_s.shape, const3),
            _resident(p_b.shape, const2),
        ],
        out_specs=pl.BlockSpec((tm, d), lambda i: (i, 0)),
        out_shape=jax.ShapeDtypeStruct((n, d), BF16),
        scratch_shapes=[pltpu.VMEM((tm, D_SGU), F32)],
        compiler_params=_params("parallel"),
        name="sgu_branch",
    )(x, g, w_sg, w_gate, ln_g, ln_b, w_s, b_s, p_b)


def _merge_kernel(x_ref, g_ref, wgate_ref, a_ref, pa_ref, mb_ref, wout_ref, post_g_ref, o_ref):
    x = x_ref[...]
    xn = (_rms(x) * g_ref[...]).astype(BF16)
    gate = jax.nn.sigmoid(_dot(xn, wgate_ref[...]))
    merged = gate * _dot(a_ref[...], pa_ref[...]) + mb_ref[...].astype(F32)
    y = _dot(merged.astype(BF16), wout_ref[...])
    o_ref[...] = x + _rms(y) * post_g_ref[...]


def _merge(x, g, w_gate, a, p_a, mb, w_out, post_g):
    n, d = x.shape
    tm = MIX_TM
    const2 = lambda i: (0, 0)
    return pl.pallas_call(
        _merge_kernel,
        grid=(n // tm,),
        in_specs=[
            pl.BlockSpec((tm, d), lambda i: (i, 0)),
            pl.BlockSpec((1, d), const2),
            _resident(w_gate.shape, const2),
            pl.BlockSpec((tm, a.shape[1]), lambda i: (i, 0)),
            _resident(p_a.shape, const2),
            pl.BlockSpec((tm, d), lambda i: (i, 0)),
            _resident(w_out.shape, const2),
            pl.BlockSpec((1, d), const2),
        ],
        out_specs=pl.BlockSpec((tm, d), lambda i: (i, 0)),
        out_shape=jax.ShapeDtypeStruct((n, d), F32),
        compiler_params=_params("parallel"),
        name="merge",
    )(x, g, w_gate, a, p_a, mb, w_out, post_g)


def _filter_positions(seq):
    t = jnp.linspace(0.0, 1.0, seq, dtype=F32)[:, None]
    w = (2.0 * math.pi / seq) * jnp.arange(seq, dtype=F32)[:, None]
    f = jnp.linspace(1e-4, FILTER_BANDS - 1, FILTER_BANDS, dtype=F32)[None, :]
    z = jnp.concatenate([t, jnp.cos(f * w), -jnp.sin(f * w)], axis=-1)
    return jnp.pad(z, ((0, 0), (0, LANES - FILTER_EMB)))


def _decay_rates():
    max_decay = math.log(DECAY_TARGET) / DECAY_FAST_PCT
    min_decay = math.log(DECAY_TARGET) / DECAY_SLOW_PCT
    return jnp.abs(jnp.linspace(min_decay, max_decay, D_HYENA, dtype=F32))[None, :]


def _dft_matrices(seq):
    n = jnp.arange(seq, dtype=jnp.int32)
    kn = (n[:, None] * n[None, :]) & (2 * seq - 1)
    ang = kn.astype(F32) * (math.pi / seq)
    c = jnp.cos(ang)
    s = jnp.sin(ang)
    alt = jnp.where((n & 1) == 1, -1.0, 1.0).astype(F32)
    s_fwd = s.at[0, :].set(alt)
    s_inv = s.at[:, 0].set(alt)
    m_inv = jnp.concatenate([c, s_inv], axis=1)
    return c.astype(BF16), s_fwd.astype(BF16), m_inv.astype(BF16)


def kernel(x, ffn1_pre_g, ffn1_w_gate, ffn1_w_up, ffn1_w_down, ffn1_post_g, mix_pre_g, w_in, hy_conv_w, hy_conv_b, filt_w1, filt_b, filt_freq, filt_w_inner, filt_w_out, hy_skip, sgu_ln_g, sgu_ln_b, sgu_w_s, sgu_b_s, p_a, p_b, w_out, mix_post_g, ffn2_pre_g, ffn2_w_gate, ffn2_w_up, ffn2_w_down, ffn2_post_g):
    bsz, seq, d = x.shape
    depth = w_in.shape[0]
    n_hy = (HYENA_ORDER + 1) * D_HYENA
    n_sg = 2 * D_SGU

    z = _filter_positions(seq)
    abs_delta = _decay_rates()
    cm, sf, minv = _dft_matrices(seq)
    row = lambda v: v[None, :]

    xs = x.reshape(bsz * seq, d)
    for i in range(depth):
        xs = _ffn(xs, row(ffn1_pre_g[i]), ffn1_w_gate[i].astype(BF16), ffn1_w_up[i].astype(BF16),
                  ffn1_w_down[i].astype(BF16), row(ffn1_post_g[i]))

        w = w_in[i].astype(BF16)
        w_hy, w_sg = w[:, :n_hy], w[:, n_hy:n_hy + n_sg]
        w_ga, w_gb = w[:, n_hy + n_sg:n_hy + n_sg + d], w[:, n_hy + n_sg + d:]
        pre_g = row(mix_pre_g[i])

        hy = _proj(xs, pre_g, w_hy).reshape(bsz, seq, n_hy)
        u = _short_conv(hy, hy_conv_w[i], row(hy_conv_b[i]))
        w1 = jnp.pad(filt_w1[i], ((0, LANES - FILTER_EMB), (0, 0)))
        taps = _filter_taps(z, w1, filt_b[i], filt_freq[i], filt_w_inner[i], filt_w_out[i], abs_delta)
        kr, ki = _filter_spectrum(cm, sf, taps)
        skip = hy_skip[i][:, None, :]
        y, y_block = u, 0
        for o in range(HYENA_ORDER):
            zf = _spectral(cm, sf, y, y_block, kr, ki, o).reshape(bsz, 2 * seq, D_HYENA)
            y = _inverse_gate(minv, zf, u, o + 1, y, y_block, skip, o)
            y_block = 0
        a = y.reshape(bsz * seq, D_HYENA)

        b_s = jnp.broadcast_to(sgu_b_s[i][:, :, None], (SGU_GROUPS, SGU_CHUNK, D_SGU // SGU_GROUPS))
        mb = _sgu(xs, pre_g, w_sg, w_gb, row(sgu_ln_g[i]), row(sgu_ln_b[i]),
                  sgu_w_s[i].astype(BF16), b_s, p_b[i].astype(BF16))

        xs = _merge(xs, pre_g, w_ga, a, p_a[i].astype(BF16), mb, w_out[i].astype(BF16),
                    row(mix_post_g[i]))

        xs = _ffn(xs, row(ffn2_pre_g[i]), ffn2_w_gate[i].astype(BF16), ffn2_w_up[i].astype(BF16),
                  ffn2_w_down[i].astype(BF16), row(ffn2_post_g[i]))
    return xs.reshape(bsz, seq, d)
```

```python
import functools
import math

import jax
import jax.numpy as jnp
from jax import lax
from jax.experimental import pallas as pl
from jax.experimental.pallas import tpu as pltpu

F32 = jnp.float32
BF16 = jnp.bfloat16

D_MODEL = 2048
D_FF = 5632
D_HYENA = D_MODEL // 2
HYENA_ORDER = 2
SHORT_CONV = 3
FILTER_BANDS = 16
FILTER_EMB = 1 + 2 * FILTER_BANDS
FILTER_HIDDEN = 64
FILTER_INNER = 2
DECAY_FAST_PCT = 0.3
DECAY_SLOW_PCT = 1.5
DECAY_TARGET = 1e-2
N_DIRS = 2
D_SGU = D_MODEL // 2
SGU_CHUNK = 128
SGU_GROUPS = 8
NORM_EPS = 1e-6
LN_EPS = 1e-5

LANES = 128
SUBLANES = 8
VMEM_LIMIT_BYTES = 56 * 1024 * 1024

FFN_TM = 512
FFN_TF = 512
PROJ_TM = 512
MIX_TM = 256
FILT_TM = 512
DFT_TM = 512
FSPEC_TM = 256
DFT_ROW_SPLIT = 64


def _params(*sem):
    return pltpu.CompilerParams(dimension_semantics=sem, vmem_limit_bytes=VMEM_LIMIT_BYTES)


def _resident(shape, index_map):
    return pl.BlockSpec(shape, index_map, pipeline_mode=pl.Buffered(1))


def _rms(x):
    return x * lax.rsqrt(jnp.mean(x * x, axis=-1, keepdims=True) + NORM_EPS)


def _dot(a, b):
    return jnp.dot(a, b, preferred_element_type=F32)


def _ffn_kernel(x_ref, pre_g_ref, wg_ref, wu_ref, wd_ref, post_g_ref, o_ref, xn_ref, acc_ref):
    j = pl.program_id(1)

    @pl.when(j == 0)
    def _():
        xn_ref[...] = (_rms(x_ref[...]) * pre_g_ref[...]).astype(BF16)
        acc_ref[...] = jnp.zeros_like(acc_ref)

    xn = xn_ref[...]
    g = _dot(xn, wg_ref[...])
    u = _dot(xn, wu_ref[...])
    h = (g * jax.nn.sigmoid(g) * u).astype(BF16)
    acc_ref[...] += _dot(h, wd_ref[...])

    @pl.when(j == pl.num_programs(1) - 1)
    def _():
        o_ref[...] = x_ref[...] + 0.5 * (_rms(acc_ref[...]) * post_g_ref[...])


def _ffn(x, pre_g, wg, wu, wd, post_g):
    n, d = x.shape
    f = wg.shape[1]
    tm, tf = FFN_TM, FFN_TF
    return pl.pallas_call(
        _ffn_kernel,
        grid=(n // tm, f // tf),
        in_specs=[
            pl.BlockSpec((tm, d), lambda i, j: (i, 0)),
            pl.BlockSpec((1, d), lambda i, j: (0, 0)),
            pl.BlockSpec((d, tf), lambda i, j: (0, j)),
            pl.BlockSpec((d, tf), lambda i, j: (0, j)),
            pl.BlockSpec((tf, d), lambda i, j: (j, 0)),
            pl.BlockSpec((1, d), lambda i, j: (0, 0)),
        ],
        out_specs=pl.BlockSpec((tm, d), lambda i, j: (i, 0)),
        out_shape=jax.ShapeDtypeStruct((n, d), F32),
        scratch_shapes=[pltpu.VMEM((tm, d), BF16), pltpu.VMEM((tm, d), F32)],
        compiler_params=_params("parallel", "arbitrary"),
        name="ffn",
    )(x, pre_g, wg, wu, wd, post_g)


def _proj_conv_kernel(xp_ref, x_ref, xn_ref, g_ref, w_ref, cw_ref, cb_ref, o_ref, *, tiles_per_seq):
    tm = x_ref.shape[0]
    pos = lax.rem(pl.program_id(0), tiles_per_seq)
    keep_prev = (pos != 0).astype(F32)
    keep_next = (pos != tiles_per_seq - 1).astype(F32)
    xa = jnp.concatenate([xp_ref[...] * keep_prev, x_ref[...], xn_ref[...] * keep_next], axis=0)
    p = _dot((_rms(xa) * g_ref[...]).astype(BF16), w_ref[...])
    rows = p.shape[0]
    mid = slice(SUBLANES, SUBLANES + tm)
    prev = pltpu.roll(p, 1, 0)[mid]
    nxt = pltpu.roll(p, rows - 1, 0)[mid]
    cw = cw_ref[...]
    o_ref[...] = (cb_ref[...] + prev * cw[0:1] + p[mid] * cw[1:2] + nxt * cw[2:3]).astype(o_ref.dtype)


def _proj_conv(x, g, w, conv_w, conv_b, seq):
    n, d = x.shape
    c = w.shape[1]
    tm = PROJ_TM
    halo_per_tile = tm // SUBLANES
    last_halo = n // SUBLANES - 1
    const2 = lambda i: (0, 0)
    return pl.pallas_call(
        functools.partial(_proj_conv_kernel, tiles_per_seq=seq // tm),
        grid=(n // tm,),
        in_specs=[
            pl.BlockSpec((SUBLANES, d), lambda i: (jnp.maximum(i * halo_per_tile - 1, 0), 0)),
            pl.BlockSpec((tm, d), lambda i: (i, 0)),
            pl.BlockSpec((SUBLANES, d), lambda i: (jnp.minimum((i + 1) * halo_per_tile, last_halo), 0)),
            pl.BlockSpec((1, d), const2),
            _resident((d, c), const2),
            pl.BlockSpec((SHORT_CONV, c), const2),
            pl.BlockSpec((1, c), const2),
        ],
        out_specs=pl.BlockSpec((tm, c), lambda i: (i, 0)),
        out_shape=jax.ShapeDtypeStruct((n, c), BF16),
        compiler_params=_params("parallel"),
        name="hyena_proj_conv",
    )(x, x, x, g, w, conv_w, conv_b)


def _filter_kernel(z_ref, w1_ref, b_ref, freq_ref, wi_ref, wo_ref, delta_ref, o_ref):
    hp = functools.partial(jnp.dot, preferred_element_type=F32, precision=lax.Precision.HIGHEST)
    z = z_ref[...]
    b = b_ref[...]
    freq = freq_ref[...]
    h = jnp.sin(freq[0:1] * (hp(z, w1_ref[...]) + b[0:1]))
    for j in range(FILTER_INNER):
        h = jnp.sin(freq[j + 1:j + 2] * (hp(h, wi_ref[j]) + b[j + 1:j + 2]))
    h = hp(h, wo_ref[...])
    t = z[:, 0:1]
    decay = jnp.exp(-t * delta_ref[...])
    tm = z.shape[0]
    row = pl.program_id(0) * tm + lax.broadcasted_iota(jnp.int32, (tm, D_HYENA), 0)
    for blk in range(HYENA_ORDER * N_DIRS):
        hb = h[:, blk * D_HYENA:(blk + 1) * D_HYENA] * decay
        if blk % N_DIRS == 1:
            hb = jnp.where(row == 0, 0.0, hb)
        o_ref[:, blk * D_HYENA:(blk + 1) * D_HYENA] = hb.astype(o_ref.dtype)


def _filter_taps(z, w1, b, freq, w_inner, w_out, abs_delta):
    seq = z.shape[0]
    c = w_out.shape[1]
    tm = FILT_TM
    full = lambda a: pl.BlockSpec(a.shape, lambda i: (0,) * a.ndim)
    return pl.pallas_call(
        _filter_kernel,
        grid=(seq // tm,),
        in_specs=[pl.BlockSpec((tm, z.shape[1]), lambda i: (i, 0)),
                  full(w1), full(b), full(freq), full(w_inner), full(w_out), full(abs_delta)],
        out_specs=pl.BlockSpec((tm, c), lambda i: (i, 0)),
        out_shape=jax.ShapeDtypeStruct((seq, c), BF16),
        compiler_params=_params("parallel"),
        name="filter_taps",
    )(z, w1, b, freq, w_inner, w_out, abs_delta)


def _filter_spectrum_kernel(c_ref, s_ref, h_ref, kr_ref, ki_ref):
    h = h_ref[...]
    a = _dot(c_ref[...], h)
    b = _dot(s_ref[...], h)
    scale = 1.0 / h.shape[0]
    af, ab = a[:, :D_HYENA], a[:, D_HYENA:]
    bf, bb = b[:, :D_HYENA], b[:, D_HYENA:]
    kr_ref[0] = (af + ab) * scale
    ki_ref[0] = (bb - bf) * scale

    @pl.when(pl.program_id(1) == 0)
    def _():
        first = lax.broadcasted_iota(jnp.int32, (SUBLANES, D_HYENA), 0) == 0
        kr8 = (af[:SUBLANES] + ab[:SUBLANES]) * scale
        kr_ref[0, :SUBLANES, :] = jnp.where(first, 0.5 * kr8, kr8)
        ki_ref[0, :SUBLANES, :] = jnp.where(
            first, 0.5 * scale * (bf[:SUBLANES] + bb[:SUBLANES]), (bb[:SUBLANES] - bf[:SUBLANES]) * scale)


def _filter_spectrum(cm, sf, taps):
    seq = cm.shape[0]
    tm = FSPEC_TM
    out = jax.ShapeDtypeStruct((HYENA_ORDER, seq, D_HYENA), F32)
    return pl.pallas_call(
        _filter_spectrum_kernel,
        grid=(HYENA_ORDER, seq // tm),
        in_specs=[
            pl.BlockSpec((tm, seq), lambda o, i: (i, 0)),
            pl.BlockSpec((tm, seq), lambda o, i: (i, 0)),
            _resident((seq, N_DIRS * D_HYENA), lambda o, i: (0, o)),
        ],
        out_specs=[pl.BlockSpec((1, tm, D_HYENA), lambda o, i: (o, i, 0))] * 2,
        out_shape=[out, out],
        compiler_params=_params("parallel", "arbitrary"),
        name="filter_spectrum",
    )(cm, sf, taps)


def _spectral_kernel(c_ref, s_ref, y_ref, kr_ref, ki_ref, z_ref):
    y = y_ref[0]
    a = _dot(c_ref[...], y)
    b = _dot(s_ref[...], y)
    kr = kr_ref[0]
    ki = ki_ref[0]
    z_ref[0, 0] = (a * kr + b * ki).astype(z_ref.dtype)
    z_ref[0, 1] = (b * kr - a * ki).astype(z_ref.dtype)

    @pl.when(pl.program_id(1) == 0)
    def _():
        first = lax.broadcasted_iota(jnp.int32, (SUBLANES, D_HYENA), 0) == 0
        a8, b8, kr8, ki8 = a[:SUBLANES], b[:SUBLANES], kr[:SUBLANES], ki[:SUBLANES]
        z_ref[0, 0, :SUBLANES, :] = jnp.where(first, a8 * kr8, a8 * kr8 + b8 * ki8).astype(z_ref.dtype)
        z_ref[0, 1, :SUBLANES, :] = jnp.where(first, b8 * ki8, b8 * kr8 - a8 * ki8).astype(z_ref.dtype)


def _spectral(cm, sf, y, col_block, kr, ki, order):
    bsz, seq, _ = y.shape
    tm = DFT_TM
    return pl.pallas_call(
        _spectral_kernel,
        grid=(bsz, seq // tm),
        in_specs=[
            pl.BlockSpec((tm, seq), lambda b, i: (i, 0)),
            pl.BlockSpec((tm, seq), lambda b, i: (i, 0)),
            _resident((1, seq, D_HYENA), lambda b, i: (b, 0, col_block)),
            pl.BlockSpec((1, tm, D_HYENA), lambda b, i: (order, i, 0)),
            pl.BlockSpec((1, tm, D_HYENA), lambda b, i: (order, i, 0)),
        ],
        out_specs=pl.BlockSpec((1, 2, tm, D_HYENA), lambda b, i: (b, 0, i, 0)),
        out_shape=jax.ShapeDtypeStruct((bsz, 2, seq, D_HYENA), BF16),
        compiler_params=_params("parallel", "arbitrary"),
        name="hyena_spectral",
    )(cm, sf, y, kr, ki)


def _inverse_gate_kernel(m_ref, z_ref, gate_ref, y_ref, skip_ref, o_ref):
    conv = _dot(m_ref[...], z_ref[0])
    y = y_ref[0].astype(F32)
    o_ref[0] = (gate_ref[0].astype(F32) * (conv + y * skip_ref[0])).astype(o_ref.dtype)


def _inverse_gate(minv, z, gate, gate_block, y, y_block, skip, order):
    bsz, two_seq, _ = z.shape
    seq = two_seq // 2
    tm = DFT_TM
    return pl.pallas_call(
        _inverse_gate_kernel,
        grid=(bsz, seq // tm),
        in_specs=[
            pl.BlockSpec((tm, two_seq), lambda b, i: (i, 0)),
            _resident((1, two_seq, D_HYENA), lambda b, i: (b, 0, 0)),
            pl.BlockSpec((1, tm, D_HYENA), lambda b, i: (b, i, gate_block)),
            pl.BlockSpec((1, tm, D_HYENA), lambda b, i: (b, i, y_block)),
            pl.BlockSpec((1, 1, D_HYENA), lambda b, i: (order, 0, 0)),
        ],
        out_specs=pl.BlockSpec((1, tm, D_HYENA), lambda b, i: (b, i, 0)),
        out_shape=jax.ShapeDtypeStruct((bsz, seq, D_HYENA), BF16),
        compiler_params=_params("parallel", "arbitrary"),
        name="hyena_inverse_gate",
    )(minv, z, gate, y, skip)


def _sgu_kernel(x_ref, g_ref, wsg_ref, wgate_ref, lng_ref, lnb_ref, ws_ref, bs_ref, pb_ref,
                o_ref, s_ref):
    xn = (_rms(x_ref[...]) * g_ref[...]).astype(BF16)
    zz = jax.nn.gelu(_dot(xn, wsg_ref[...]), approximate=True)
    u = zz[:, :D_SGU]
    v = zz[:, D_SGU:]
    mu = jnp.mean(v, axis=-1, keepdims=True)
    vc = v - mu
    var = jnp.mean(vc * vc, axis=-1, keepdims=True)
    v = (vc * lax.rsqrt(var + LN_EPS) * lng_ref[...] + lnb_ref[...]).astype(BF16)
    dg = D_SGU // SGU_GROUPS
    for c in range(x_ref.shape[0] // SGU_CHUNK):
        rows = slice(c * SGU_CHUNK, (c + 1) * SGU_CHUNK)
        for grp in range(SGU_GROUPS):
            cols = slice(grp * dg, (grp + 1) * dg)
            s_ref[rows, cols] = _dot(ws_ref[grp], v[rows, cols]) + bs_ref[grp]
    gated = (u * s_ref[...]).astype(BF16)
    branch = _dot(gated, pb_ref[...])
    gate = jax.nn.sigmoid(_dot(xn, wgate_ref[...]))
    o_ref[...] = (gate * branch).astype(o_ref.dtype)


def _sgu(x, g, w_sg, w_gate, ln_g, ln_b, w_s, b_s, p_b):
    n, d = x.shape
    tm = MIX_TM
    const2 = lambda i: (0, 0)
    const3 = lambda i: (0, 0, 0)
    return pl.pallas_call(
        _sgu_kernel,
        grid=(n // tm,),
        in_specs=[
            pl.BlockSpec((tm, d), lambda i: (i, 0)),
            pl.BlockSpec((1, d), const2),
            _resident(w_sg.shape, const2),
            _resident(w_gate.shape, const2),
            pl.BlockSpec(ln_g.shape, const2),
            pl.BlockSpec(ln_b.shape, const2),
            _resident(w_s.shape, const3),
            _resident(b_s.shape, const3),
            _resident(p_b.shape, const2),
        ],
        out_specs=pl.BlockSpec((tm, d), lambda i: (i, 0)),
        out_shape=jax.ShapeDtypeStruct((n, d), BF16),
        scratch_shapes=[pltpu.VMEM((tm, D_SGU), F32)],
        compiler_params=_params("parallel"),
        name="sgu_branch",
    )(x, g, w_sg, w_gate, ln_g, ln_b, w_s, b_s, p_b)


def _merge_kernel(x_ref, g_ref, wgate_ref, a_ref, pa_ref, mb_ref, wout_ref, post_g_ref, o_ref):
    x = x_ref[...]
    xn = (_rms(x) * g_ref[...]).astype(BF16)
    gate = jax.nn.sigmoid(_dot(xn, wgate_ref[...]))
    merged = gate * _dot(a_ref[...], pa_ref[...]) + mb_ref[...].astype(F32)
    y = _dot(merged.astype(BF16), wout_ref[...])
    o_ref[...] = x + _rms(y) * post_g_ref[...]


def _merge(x, g, w_gate, a, p_a, mb, w_out, post_g):
    n, d = x.shape
    tm = MIX_TM
    const2 = lambda i: (0, 0)
    return pl.pallas_call(
        _merge_kernel,
        grid=(n // tm,),
        in_specs=[
            pl.BlockSpec((tm, d), lambda i: (i, 0)),
            pl.BlockSpec((1, d), const2),
            _resident(w_gate.shape, const2),
            pl.BlockSpec((tm, a.shape[1]), lambda i: (i, 0)),
            _resident(p_a.shape, const2),
            pl.BlockSpec((tm, d), lambda i: (i, 0)),
            _resident(w_out.shape, const2),
            pl.BlockSpec((1, d), const2),
        ],
        out_specs=pl.BlockSpec((tm, d), lambda i: (i, 0)),
        out_shape=jax.ShapeDtypeStruct((n, d), F32),
        compiler_params=_params("parallel"),
        name="merge",
    )(x, g, w_gate, a, p_a, mb, w_out, post_g)


def _filter_positions(seq):
    t = jnp.linspace(0.0, 1.0, seq, dtype=F32)[:, None]
    w = (2.0 * math.pi / seq) * jnp.arange(seq, dtype=F32)[:, None]
    f = jnp.linspace(1e-4, FILTER_BANDS - 1, FILTER_BANDS, dtype=F32)[None, :]
    z = jnp.concatenate([t, jnp.cos(f * w), -jnp.sin(f * w)], axis=-1)
    return jnp.pad(z, ((0, 0), (0, LANES - FILTER_EMB)))


def _decay_rates():
    max_decay = math.log(DECAY_TARGET) / DECAY_FAST_PCT
    min_decay = math.log(DECAY_TARGET) / DECAY_SLOW_PCT
    return jnp.abs(jnp.linspace(min_decay, max_decay, D_HYENA, dtype=F32))[None, :]


def _dft_matrices(seq):
    n = jnp.arange(seq, dtype=jnp.int32)
    blk = DFT_ROW_SPLIT
    assert seq % blk == 0

    def angles(k):
        return ((k[:, None] * n[None, :]) & (2 * seq - 1)).astype(F32) * (math.pi / seq)

    a_hi = angles(blk * jnp.arange(seq // blk, dtype=jnp.int32))[:, None, :]
    a_lo = angles(jnp.arange(blk, dtype=jnp.int32))[None, :, :]
    c = (jnp.cos(a_hi) * jnp.cos(a_lo) - jnp.sin(a_hi) * jnp.sin(a_lo)).reshape(seq, seq)
    s = (jnp.sin(a_hi) * jnp.cos(a_lo) + jnp.cos(a_hi) * jnp.sin(a_lo)).reshape(seq, seq)
    alt = jnp.where((n & 1) == 1, -1.0, 1.0).astype(F32)
    s_fwd = s.at[0, :].set(alt)
    s_inv = s.at[:, 0].set(alt)
    m_inv = jnp.concatenate([c, s_inv], axis=1)
    return c.astype(BF16), s_fwd.astype(BF16), m_inv.astype(BF16)


def kernel(x, ffn1_pre_g, ffn1_w_gate, ffn1_w_up, ffn1_w_down, ffn1_post_g, mix_pre_g, w_in, hy_conv_w, hy_conv_b, filt_w1, filt_b, filt_freq, filt_w_inner, filt_w_out, hy_skip, sgu_ln_g, sgu_ln_b, sgu_w_s, sgu_b_s, p_a, p_b, w_out, mix_post_g, ffn2_pre_g, ffn2_w_gate, ffn2_w_up, ffn2_w_down, ffn2_post_g):
    bsz, seq, d = x.shape
    depth = w_in.shape[0]
    n_hy = (HYENA_ORDER + 1) * D_HYENA
    n_sg = 2 * D_SGU

    z = _filter_positions(seq)
    abs_delta = _decay_rates()
    cm, sf, minv = _dft_matrices(seq)
    row = lambda v: v[None, :]

    xs = x.reshape(bsz * seq, d)
    for i in range(depth):
        xs = _ffn(xs, row(ffn1_pre_g[i]), ffn1_w_gate[i].astype(BF16), ffn1_w_up[i].astype(BF16),
                  ffn1_w_down[i].astype(BF16), row(ffn1_post_g[i]))

        w_cols = lambda lo, hi: w_in[i, :, lo:hi].astype(BF16)
        w_hy, w_sg = w_cols(0, n_hy), w_cols(n_hy, n_hy + n_sg)
        w_ga, w_gb = w_cols(n_hy + n_sg, n_hy + n_sg + d), w_cols(n_hy + n_sg + d, n_hy + n_sg + 2 * d)
        pre_g = row(mix_pre_g[i])

        u = _proj_conv(xs, pre_g, w_hy, hy_conv_w[i], row(hy_conv_b[i]), seq)
        u = u.reshape(bsz, seq, n_hy)
        w1 = jnp.pad(filt_w1[i], ((0, LANES - FILTER_EMB), (0, 0)))
        taps = _filter_taps(z, w1, filt_b[i], filt_freq[i], filt_w_inner[i], filt_w_out[i], abs_delta)
        kr, ki = _filter_spectrum(cm, sf, taps)
        skip = hy_skip[i][:, None, :]
        y, y_block = u, 0
        for o in range(HYENA_ORDER):
            zf = _spectral(cm, sf, y, y_block, kr, ki, o).reshape(bsz, 2 * seq, D_HYENA)
            y = _inverse_gate(minv, zf, u, o + 1, y, y_block, skip, o)
            y_block = 0
        a = y.reshape(bsz * seq, D_HYENA)

        b_s = jnp.broadcast_to(sgu_b_s[i][:, :, None], (SGU_GROUPS, SGU_CHUNK, D_SGU // SGU_GROUPS))
        mb = _sgu(xs, pre_g, w_sg, w_gb, row(sgu_ln_g[i]), row(sgu_ln_b[i]),
                  sgu_w_s[i].astype(BF16), b_s, p_b[i].astype(BF16))

        xs = _merge(xs, pre_g, w_ga, a, p_a[i].astype(BF16), mb, w_out[i].astype(BF16),
                    row(mix_post_g[i]))

        xs = _ffn(xs, row(ffn2_pre_g[i]), ffn2_w_gate[i].astype(BF16), ffn2_w_up[i].astype(BF16),
                  ffn2_w_down[i].astype(BF16), row(ffn2_post_g[i]))
    return xs.reshape(bsz, seq, d)
```

```python
import functools
import math

import jax
import jax.numpy as jnp
from jax import lax
from jax.experimental import pallas as pl
from jax.experimental.pallas import tpu as pltpu

F32 = jnp.float32
BF16 = jnp.bfloat16

D_MODEL = 2048
D_FF = 5632
D_HYENA = D_MODEL // 2
HYENA_ORDER = 2
SHORT_CONV = 3
FILTER_BANDS = 16
FILTER_EMB = 1 + 2 * FILTER_BANDS
FILTER_HIDDEN = 64
FILTER_INNER = 2
DECAY_FAST_PCT = 0.3
DECAY_SLOW_PCT = 1.5
DECAY_TARGET = 1e-2
N_DIRS = 2
D_SGU = D_MODEL // 2
SGU_CHUNK = 128
SGU_GROUPS = 8
NORM_EPS = 1e-6
LN_EPS = 1e-5

LANES = 128
SUBLANES = 8
VMEM_LIMIT_BYTES = 56 * 1024 * 1024

FFN_TM = 512
FFN_TF = 512
PROJ_TM = 512
MIX_TM = 256
FILT_TM = 512
FFT_N2 = 128
FFT_TN = 8192
W_IN_BLOCK = 1024


def _params(*sem):
    return pltpu.CompilerParams(dimension_semantics=sem, vmem_limit_bytes=VMEM_LIMIT_BYTES)


def _resident(shape, index_map):
    return pl.BlockSpec(shape, index_map, pipeline_mode=pl.Buffered(1))


def _rms(x):
    return x * lax.rsqrt(jnp.mean(x * x, axis=-1, keepdims=True) + NORM_EPS)


def _dot(a, b):
    return jnp.dot(a, b, preferred_element_type=F32)


def _ffn_kernel(x_ref, pre_g_ref, wg_ref, wu_ref, wd_ref, post_g_ref, o_ref, xn_ref, acc_ref):
    j = pl.program_id(1)

    @pl.when(j == 0)
    def _():
        xn_ref[...] = (_rms(x_ref[...]) * pre_g_ref[...]).astype(BF16)
        acc_ref[...] = jnp.zeros_like(acc_ref)

    xn = xn_ref[...]
    g = _dot(xn, wg_ref[...])
    u = _dot(xn, wu_ref[...])
    h = (g * jax.nn.sigmoid(g) * u).astype(BF16)
    acc_ref[...] += _dot(h, wd_ref[...])

    @pl.when(j == pl.num_programs(1) - 1)
    def _():
        o_ref[...] = x_ref[...] + 0.5 * (_rms(acc_ref[...]) * post_g_ref[...])


def _ffn(x, layer, pre_g, wg, wu, wd, post_g):
    n, d = x.shape
    f = wg.shape[2]
    tm, tf = FFN_TM, FFN_TF
    return pl.pallas_call(
        _ffn_kernel,
        grid=(n // tm, f // tf),
        in_specs=[
            pl.BlockSpec((tm, d), lambda i, j: (i, 0)),
            pl.BlockSpec((1, d), lambda i, j: (0, 0)),
            pl.BlockSpec((None, d, tf), lambda i, j: (layer, 0, j)),
            pl.BlockSpec((None, d, tf), lambda i, j: (layer, 0, j)),
            pl.BlockSpec((None, tf, d), lambda i, j: (layer, j, 0)),
            pl.BlockSpec((1, d), lambda i, j: (0, 0)),
        ],
        out_specs=pl.BlockSpec((tm, d), lambda i, j: (i, 0)),
        out_shape=jax.ShapeDtypeStruct((n, d), F32),
        scratch_shapes=[pltpu.VMEM((tm, d), BF16), pltpu.VMEM((tm, d), F32)],
        compiler_params=_params("parallel", "arbitrary"),
        name="ffn",
    )(x, pre_g, wg, wu, wd, post_g)


def _proj_conv_kernel(xp_ref, x_ref, xn_ref, g_ref, w_ref, cw_ref, cb_ref, o_ref, *, tiles_per_seq):
    tm = x_ref.shape[0]
    pos = lax.rem(pl.program_id(0), tiles_per_seq)
    keep_prev = (pos != 0).astype(F32)
    keep_next = (pos != tiles_per_seq - 1).astype(F32)
    xa = jnp.concatenate([xp_ref[...] * keep_prev, x_ref[...], xn_ref[...] * keep_next], axis=0)
    p = _dot((_rms(xa) * g_ref[...]).astype(BF16), w_ref[...])
    rows = p.shape[0]
    mid = slice(SUBLANES, SUBLANES + tm)
    prev = pltpu.roll(p, 1, 0)[mid]
    nxt = pltpu.roll(p, rows - 1, 0)[mid]
    cw = cw_ref[...]
    y = cb_ref[...] + prev * cw[0:1] + p[mid] * cw[1:2] + nxt * cw[2:3]
    for part in range(o_ref.shape[0]):
        o_ref[part] = y[:, part * D_HYENA:(part + 1) * D_HYENA].astype(o_ref.dtype)


def _proj_conv(x, layer, g, w_in, conv_w, conv_b, seq):
    n, d = x.shape
    parts = HYENA_ORDER + 1
    c = parts * D_HYENA
    tm = PROJ_TM
    halo_per_tile = tm // SUBLANES
    last_halo = n // SUBLANES - 1
    const2 = lambda i: (0, 0)
    return pl.pallas_call(
        functools.partial(_proj_conv_kernel, tiles_per_seq=seq // tm),
        grid=(n // tm,),
        in_specs=[
            pl.BlockSpec((SUBLANES, d), lambda i: (jnp.maximum(i * halo_per_tile - 1, 0), 0)),
            pl.BlockSpec((tm, d), lambda i: (i, 0)),
            pl.BlockSpec((SUBLANES, d), lambda i: (jnp.minimum((i + 1) * halo_per_tile, last_halo), 0)),
            pl.BlockSpec((1, d), const2),
            _resident((None, d, c), lambda i: (layer, 0, 0)),
            pl.BlockSpec((SHORT_CONV, c), const2),
            pl.BlockSpec((1, c), const2),
        ],
        out_specs=pl.BlockSpec((parts, tm, D_HYENA), lambda i: (0, i, 0)),
        out_shape=jax.ShapeDtypeStruct((parts, n, D_HYENA), BF16),
        compiler_params=_params("parallel"),
        name="hyena_proj_conv",
    )(x, x, x, g, w_in, conv_w, conv_b)


def _filter_kernel(z_ref, w1_ref, b_ref, freq_ref, wi_ref, wo_ref, delta_ref, o_ref):
    hp = functools.partial(jnp.dot, preferred_element_type=F32, precision=lax.Precision.HIGHEST)
    z = z_ref[...]
    b = b_ref[...]
    freq = freq_ref[...]
    h = jnp.sin(freq[0:1] * (hp(z, w1_ref[...]) + b[0:1]))
    for j in range(FILTER_INNER):
        h = jnp.sin(freq[j + 1:j + 2] * (hp(h, wi_ref[j]) + b[j + 1:j + 2]))
    h = hp(h, wo_ref[...])
    t = z[:, 0:1]
    decay = jnp.exp(-t * delta_ref[...])
    tm = z.shape[0]
    row = pl.program_id(0) * tm + lax.broadcasted_iota(jnp.int32, (tm, D_HYENA), 0)
    for blk in range(HYENA_ORDER * N_DIRS):
        hb = h[:, blk * D_HYENA:(blk + 1) * D_HYENA] * decay
        if blk % N_DIRS == 1:
            hb = jnp.where(row == 0, 0.0, hb)
        o_ref[blk] = hb.astype(o_ref.dtype)


def _filter_taps(z, w1, b, freq, w_inner, w_out, abs_delta):
    seq = z.shape[0]
    nsig = HYENA_ORDER * N_DIRS
    tm = FILT_TM
    full = lambda a: pl.BlockSpec(a.shape, lambda i: (0,) * a.ndim)
    return pl.pallas_call(
        _filter_kernel,
        grid=(seq // tm,),
        in_specs=[pl.BlockSpec((tm, z.shape[1]), lambda i: (i, 0)),
                  full(w1), full(b), full(freq), full(w_inner), full(w_out), full(abs_delta)],
        out_specs=pl.BlockSpec((nsig, tm, D_HYENA), lambda i: (0, i, 0)),
        out_shape=jax.ShapeDtypeStruct((nsig, seq, D_HYENA), BF16),
        compiler_params=_params("parallel"),
        name="filter_taps",
    )(z, w1, b, freq, w_inner, w_out, abs_delta)


def _stage1_kernel(f_ref, x_ref, o_ref):
    o_ref[0] = _dot(f_ref[...], x_ref[0]).astype(o_ref.dtype)


def _stage1(f1, x, part):
    _, nsig, h1, width = x.shape
    n1 = f1.shape[0]
    tn = FFT_TN
    return pl.pallas_call(
        _stage1_kernel,
        grid=(nsig, width // tn),
        in_specs=[
            pl.BlockSpec((n1, h1), lambda s, j: (0, 0)),
            pl.BlockSpec((None, 1, h1, tn), lambda s, j: (part, s, 0, j)),
        ],
        out_specs=pl.BlockSpec((1, n1, tn), lambda s, j: (s, 0, j)),
        out_shape=jax.ShapeDtypeStruct((nsig, n1, width), BF16),
        compiler_params=_params("parallel", "parallel"),
        name="fft_stage1",
    )(f1, x)


def _filter_spectrum_kernel(t_ref, a_ref, k_ref, *, scale):
    t = t_ref[0]
    half = t.shape[0] // 2
    for o in range(HYENA_ORDER):
        xf = _dot(t, a_ref[N_DIRS * o, 0])
        xb = _dot(t, a_ref[N_DIRS * o + 1, 0])
        k_ref[o, 0, :half, :] = (xf[:half] + xb[:half]) * scale
        k_ref[o, 0, half:, :] = (xf[half:] - xb[half:]) * scale


def _filter_spectrum(t2, a, seq):
    nsig, h1, rows, c = a.shape
    return pl.pallas_call(
        functools.partial(_filter_spectrum_kernel, scale=1.0 / seq),
        grid=(h1,),
        in_specs=[
            pl.BlockSpec((1, rows, rows), lambda k: (k, 0, 0)),
            pl.BlockSpec((nsig, 1, rows, c), lambda k: (0, k, 0, 0)),
        ],
        out_specs=pl.BlockSpec((HYENA_ORDER, 1, rows, c), lambda k: (0, k, 0, 0)),
        out_shape=jax.ShapeDtypeStruct((HYENA_ORDER, h1, rows, c), F32),
        compiler_params=_params("parallel"),
        name="filter_spectrum",
    )(t2, a)


def _spectral_kernel(t_ref, tt_ref, a_ref, k_ref, o_ref):
    t = t_ref[0]
    tt = tt_ref[0]
    k = k_ref[0, 0]
    half = k.shape[0] // 2
    kr, ki = k[:half], k[half:]
    for b in range(a_ref.shape[0]):
        x = _dot(t, a_ref[b, 0])
        xr, xi = x[:half], x[half:]
        z = jnp.concatenate([xr * kr - xi * ki, xr * ki + xi * kr], axis=0).astype(BF16)
        o_ref[b, 0] = _dot(tt, z).astype(o_ref.dtype)


def _spectral(t2, t2t, a, k, order):
    bsz, h1, rows, c = a.shape
    return pl.pallas_call(
        _spectral_kernel,
        grid=(h1,),
        in_specs=[
            pl.BlockSpec((1, rows, rows), lambda q: (q, 0, 0)),
            pl.BlockSpec((1, rows, rows), lambda q: (q, 0, 0)),
            pl.BlockSpec((bsz, 1, rows, c), lambda q: (0, q, 0, 0)),
            pl.BlockSpec((1, 1, rows, c), lambda q: (order, q, 0, 0)),
        ],
        out_specs=pl.BlockSpec((bsz, 1, rows, c), lambda q: (0, q, 0, 0)),
        out_shape=jax.ShapeDtypeStruct((bsz, h1, rows, c), BF16),
        compiler_params=_params("parallel"),
        name="hyena_spectral",
    )(t2, t2t, a, k)


def _inverse_gate_kernel(ft_ref, b_ref, gate_ref, y_ref, skip_ref, o_ref):
    conv = _dot(ft_ref[...], b_ref[0])
    y = y_ref[0].astype(F32)
    o_ref[0, 0] = (gate_ref[0].astype(F32) * (conv + y * skip_ref[...])).astype(o_ref.dtype)


def _inverse_gate(f1t, b, gate, gate_part, y, y_part, skip_row):
    bsz, n1, width = b.shape
    h1 = f1t.shape[0]
    tn = FFT_TN
    return pl.pallas_call(
        _inverse_gate_kernel,
        grid=(bsz, width // tn),
        in_specs=[
            pl.BlockSpec((h1, n1), lambda s, j: (0, 0)),
            pl.BlockSpec((1, n1, tn), lambda s, j: (s, 0, j)),
            pl.BlockSpec((None, 1, h1, tn), lambda s, j: (gate_part, s, 0, j)),
            pl.BlockSpec((None, 1, h1, tn), lambda s, j: (y_part, s, 0, j)),
            pl.BlockSpec((1, tn), lambda s, j: (0, 0)),
        ],
        out_specs=pl.BlockSpec((1, 1, h1, tn), lambda s, j: (0, s, 0, j)),
        out_shape=jax.ShapeDtypeStruct((1, bsz, h1, width), BF16),
        compiler_params=_params("parallel", "parallel"),
        name="hyena_inverse_gate",
    )(f1t, b, gate, y, skip_row)


def _sgu_kernel(x_ref, g_ref, wu_ref, wv_ref, wg0_ref, wg1_ref, lng_ref, lnb_ref, ws_ref, bs_ref,
                pb_ref, o_ref, s_ref):
    xn = (_rms(x_ref[...]) * g_ref[...]).astype(BF16)
    u = jax.nn.gelu(_dot(xn, wu_ref[...]), approximate=True)
    v = jax.nn.gelu(_dot(xn, wv_ref[...]), approximate=True)
    mu = jnp.mean(v, axis=-1, keepdims=True)
    vc = v - mu
    var = jnp.mean(vc * vc, axis=-1, keepdims=True)
    v = (vc * lax.rsqrt(var + LN_EPS) * lng_ref[...] + lnb_ref[...]).astype(BF16)
    dg = D_SGU // SGU_GROUPS
    for c in range(x_ref.shape[0] // SGU_CHUNK):
        rows = slice(c * SGU_CHUNK, (c + 1) * SGU_CHUNK)
        for grp in range(SGU_GROUPS):
            cols = slice(grp * dg, (grp + 1) * dg)
            s_ref[rows, cols] = _dot(ws_ref[grp], v[rows, cols]) + bs_ref[grp]
    gated = (u * s_ref[...]).astype(BF16)
    branch = _dot(gated, pb_ref[...])
    for h, wg_ref in enumerate((wg0_ref, wg1_ref)):
        cols = slice(h * W_IN_BLOCK, (h + 1) * W_IN_BLOCK)
        gate = jax.nn.sigmoid(_dot(xn, wg_ref[...]))
        o_ref[:, cols] = (gate * branch[:, cols]).astype(o_ref.dtype)


def _sgu(x, layer, g, w_in, ln_g, ln_b, w_s, b_s, p_b):
    n, d = x.shape
    tm = MIX_TM
    const2 = lambda i: (0, 0)
    first = (HYENA_ORDER + 1) * D_HYENA // W_IN_BLOCK
    gate_b = first + 2 * D_SGU // W_IN_BLOCK + d // W_IN_BLOCK
    w_blk = lambda blk: _resident((None, d, W_IN_BLOCK), lambda i: (layer, 0, blk))
    return pl.pallas_call(
        _sgu_kernel,
        grid=(n // tm,),
        in_specs=[
            pl.BlockSpec((tm, d), lambda i: (i, 0)),
            pl.BlockSpec((1, d), const2),
            w_blk(first), w_blk(first + 1), w_blk(gate_b), w_blk(gate_b + 1),
            pl.BlockSpec(ln_g.shape, const2),
            pl.BlockSpec(ln_b.shape, const2),
            _resident((None,) + w_s.shape[1:], lambda i: (layer, 0, 0, 0)),
            _resident(b_s.shape, lambda i: (0, 0, 0)),
            _resident((None,) + p_b.shape[1:], lambda i: (layer, 0, 0)),
        ],
        out_specs=pl.BlockSpec((tm, d), lambda i: (i, 0)),
        out_shape=jax.ShapeDtypeStruct((n, d), BF16),
        scratch_shapes=[pltpu.VMEM((tm, D_SGU), F32)],
        compiler_params=_params("parallel"),
        name="sgu_branch",
    )(x, g, w_in, w_in, w_in, w_in, ln_g, ln_b, w_s, b_s, p_b)


def _merge_kernel(x_ref, g_ref, wg0_ref, wg1_ref, a_ref, pa_ref, mb_ref, wout_ref, post_g_ref, o_ref):
    x = x_ref[...]
    xn = (_rms(x) * g_ref[...]).astype(BF16)
    pa = _dot(a_ref[...], pa_ref[...])
    halves = []
    for h, wg_ref in enumerate((wg0_ref, wg1_ref)):
        cols = slice(h * W_IN_BLOCK, (h + 1) * W_IN_BLOCK)
        gate = jax.nn.sigmoid(_dot(xn, wg_ref[...]))
        halves.append((gate * pa[:, cols] + mb_ref[:, cols].astype(F32)).astype(BF16))
    y = _dot(jnp.concatenate(halves, axis=1), wout_ref[...])
    o_ref[...] = x + _rms(y) * post_g_ref[...]


def _merge(x, layer, g, w_in, a, p_a, mb, w_out, post_g):
    n, d = x.shape
    tm = MIX_TM
    const2 = lambda i: (0, 0)
    gate_a = ((HYENA_ORDER + 1) * D_HYENA + 2 * D_SGU) // W_IN_BLOCK
    w_blk = lambda blk: _resident((None, d, W_IN_BLOCK), lambda i: (layer, 0, blk))
    return pl.pallas_call(
        _merge_kernel,
        grid=(n // tm,),
        in_specs=[
            pl.BlockSpec((tm, d), lambda i: (i, 0)),
            pl.BlockSpec((1, d), const2),
            w_blk(gate_a), w_blk(gate_a + 1),
            pl.BlockSpec((tm, a.shape[1]), lambda i: (i, 0)),
            _resident((None,) + p_a.shape[1:], lambda i: (layer, 0, 0)),
            pl.BlockSpec((tm, d), lambda i: (i, 0)),
            _resident((None,) + w_out.shape[1:], lambda i: (layer, 0, 0)),
            pl.BlockSpec((1, d), const2),
        ],
        out_specs=pl.BlockSpec((tm, d), lambda i: (i, 0)),
        out_shape=jax.ShapeDtypeStruct((n, d), F32),
        compiler_params=_params("parallel"),
        name="merge",
    )(x, g, w_in, w_in, a, p_a, mb, w_out, post_g)


def _filter_positions(seq):
    t = jnp.linspace(0.0, 1.0, seq, dtype=F32)[:, None]
    w = (2.0 * math.pi / seq) * jnp.arange(seq, dtype=F32)[:, None]
    f = jnp.linspace(1e-4, FILTER_BANDS - 1, FILTER_BANDS, dtype=F32)[None, :]
    z = jnp.concatenate([t, jnp.cos(f * w), -jnp.sin(f * w)], axis=-1)
    return jnp.pad(z, ((0, 0), (0, LANES - FILTER_EMB)))


def _decay_rates():
    max_decay = math.log(DECAY_TARGET) / DECAY_FAST_PCT
    min_decay = math.log(DECAY_TARGET) / DECAY_SLOW_PCT
    return jnp.abs(jnp.linspace(min_decay, max_decay, D_HYENA, dtype=F32))[None, :]


def _fft_tables(seq):
    n = 2 * seq
    n2 = FFT_N2
    n1 = n // n2
    h1 = n1 // 2
    k1 = jnp.arange(h1, dtype=jnp.int32)
    m1 = ((2 * k1[:, None] + 1) * k1[None, :]) % (2 * n1)
    a1 = m1.astype(F32) * (math.pi / n1)
    f1 = jnp.stack([jnp.cos(a1), -jnp.sin(a1)], axis=1).reshape(n1, h1)
    q = jnp.arange(n2, dtype=jnp.int32)
    kk = 2 * (k1[:, None, None] + n1 * q[None, :, None]) + 1
    a2 = ((kk * q[None, None, :]) % (2 * n)).astype(F32) * (math.pi / n)
    c, s = jnp.cos(a2), jnp.sin(a2)
    t2 = jnp.concatenate([jnp.concatenate([c, s], axis=2), jnp.concatenate([-s, c], axis=2)], axis=1)
    return (f1.astype(BF16), f1.T.astype(BF16), t2.astype(BF16), jnp.swapaxes(t2, 1, 2).astype(BF16))


def kernel(x, ffn1_pre_g, ffn1_w_gate, ffn1_w_up, ffn1_w_down, ffn1_post_g, mix_pre_g, w_in, hy_conv_w, hy_conv_b, filt_w1, filt_b, filt_freq, filt_w_inner, filt_w_out, hy_skip, sgu_ln_g, sgu_ln_b, sgu_w_s, sgu_b_s, p_a, p_b, w_out, mix_post_g, ffn2_pre_g, ffn2_w_gate, ffn2_w_up, ffn2_w_down, ffn2_post_g):
    bsz, seq, d = x.shape
    depth = w_in.shape[0]
    n_tok = bsz * seq

    z = _filter_positions(seq)
    abs_delta = _decay_rates()
    f1, f1t, t2, t2t = _fft_tables(seq)
    n1, h1 = f1.shape
    width = FFT_N2 * D_HYENA
    rows2 = 2 * FFT_N2
    row = lambda v: v[None, :]
    bf = lambda w: w.astype(BF16)

    ffn1 = (bf(ffn1_w_gate), bf(ffn1_w_up), bf(ffn1_w_down))
    ffn2 = (bf(ffn2_w_gate), bf(ffn2_w_up), bf(ffn2_w_down))
    w_in, p_a, p_b, w_out, sgu_w_s = bf(w_in), bf(p_a), bf(p_b), bf(w_out), bf(sgu_w_s)

    xs = x.reshape(n_tok, d)
    for i in range(depth):
        xs = _ffn(xs, i, row(ffn1_pre_g[i]), *ffn1, row(ffn1_post_g[i]))
        pre_g = row(mix_pre_g[i])

        u = _proj_conv(xs, i, pre_g, w_in, hy_conv_w[i], row(hy_conv_b[i]), seq)
        u = u.reshape(HYENA_ORDER + 1, bsz, h1, width)
        w1 = jnp.pad(filt_w1[i], ((0, LANES - FILTER_EMB), (0, 0)))
        taps = _filter_taps(z, w1, filt_b[i], filt_freq[i], filt_w_inner[i], filt_w_out[i], abs_delta)
        taps_a = _stage1(f1, taps.reshape(1, HYENA_ORDER * N_DIRS, h1, width), 0)
        k_spec = _filter_spectrum(t2, taps_a.reshape(HYENA_ORDER * N_DIRS, h1, rows2, D_HYENA), seq)
        y, y_part = u, 0
        for o in range(HYENA_ORDER):
            a1 = _stage1(f1, y, y_part).reshape(bsz, h1, rows2, D_HYENA)
            b2 = _spectral(t2, t2t, a1, k_spec, o).reshape(bsz, n1, width)
            skip_row = jnp.tile(hy_skip[i, o], FFT_TN // D_HYENA)[None, :]
            y = _inverse_gate(f1t, b2, u, o + 1, y, y_part, skip_row)
            y_part = 0
        a = y.reshape(n_tok, D_HYENA)

        b_s = jnp.broadcast_to(sgu_b_s[i][:, :, None], (SGU_GROUPS, SGU_CHUNK, D_SGU // SGU_GROUPS))
        mb = _sgu(xs, i, pre_g, w_in, row(sgu_ln_g[i]), row(sgu_ln_b[i]), sgu_w_s, b_s, p_b)

        xs = _merge(xs, i, pre_g, w_in, a, p_a, mb, w_out, row(mix_post_g[i]))
        xs = _ffn(xs, i, row(ffn2_pre_g[i]), *ffn2, row(ffn2_post_g[i]))
    return xs.reshape(bsz, seq, d)
```

```python
import functools
import math

import jax
import jax.numpy as jnp
from jax import lax
from jax.experimental import pallas as pl
from jax.experimental.pallas import tpu as pltpu

F32 = jnp.float32
BF16 = jnp.bfloat16

D_MODEL = 2048
D_FF = 5632
D_HYENA = D_MODEL // 2
HYENA_ORDER = 2
SHORT_CONV = 3
FILTER_BANDS = 16
FILTER_EMB = 1 + 2 * FILTER_BANDS
FILTER_HIDDEN = 64
FILTER_INNER = 2
DECAY_FAST_PCT = 0.3
DECAY_SLOW_PCT = 1.5
DECAY_TARGET = 1e-2
N_DIRS = 2
D_SGU = D_MODEL // 2
SGU_CHUNK = 128
SGU_GROUPS = 8
NORM_EPS = 1e-6
LN_EPS = 1e-5

LANES = 128
SUBLANES = 8
VMEM_LIMIT_BYTES = 56 * 1024 * 1024

FFN_TM = 512
FFN_TF = 512
PROJ_TM = 512
MIX_TM = 256
FILT_TM = 512
FFT_N2 = 128
FFT_ROWS = 16
W_IN_BLOCK = 1024


def _params(*sem):
    return pltpu.CompilerParams(dimension_semantics=sem, vmem_limit_bytes=VMEM_LIMIT_BYTES)


def _resident(shape, index_map):
    return pl.BlockSpec(shape, index_map, pipeline_mode=pl.Buffered(1))


def _rms(x):
    return x * lax.rsqrt(jnp.mean(x * x, axis=-1, keepdims=True) + NORM_EPS)


def _dot(a, b):
    return jnp.dot(a, b, preferred_element_type=F32)


def _ffn_kernel(x_ref, pre_g_ref, wg_ref, wu_ref, wd_ref, post_g_ref, o_ref, xn_ref, acc_ref):
    j = pl.program_id(1)

    @pl.when(j == 0)
    def _():
        xn_ref[...] = (_rms(x_ref[...]) * pre_g_ref[...]).astype(BF16)
        acc_ref[...] = jnp.zeros_like(acc_ref)

    xn = xn_ref[...]
    g = _dot(xn, wg_ref[...])
    u = _dot(xn, wu_ref[...])
    h = (g * jax.nn.sigmoid(g) * u).astype(BF16)
    acc_ref[...] += _dot(h, wd_ref[...])

    @pl.when(j == pl.num_programs(1) - 1)
    def _():
        o_ref[...] = x_ref[...] + 0.5 * (_rms(acc_ref[...]) * post_g_ref[...])


def _ffn(x, layer, pre_g, wg, wu, wd, post_g):
    n, d = x.shape
    f = wg.shape[2]
    tm, tf = FFN_TM, FFN_TF
    return pl.pallas_call(
        _ffn_kernel,
        grid=(n // tm, f // tf),
        in_specs=[
            pl.BlockSpec((tm, d), lambda i, j: (i, 0)),
            pl.BlockSpec((1, d), lambda i, j: (0, 0)),
            pl.BlockSpec((None, d, tf), lambda i, j: (layer, 0, j)),
            pl.BlockSpec((None, d, tf), lambda i, j: (layer, 0, j)),
            pl.BlockSpec((None, tf, d), lambda i, j: (layer, j, 0)),
            pl.BlockSpec((1, d), lambda i, j: (0, 0)),
        ],
        out_specs=pl.BlockSpec((tm, d), lambda i, j: (i, 0)),
        out_shape=jax.ShapeDtypeStruct((n, d), F32),
        scratch_shapes=[pltpu.VMEM((tm, d), BF16), pltpu.VMEM((tm, d), F32)],
        compiler_params=_params("parallel", "arbitrary"),
        name="ffn",
    )(x, pre_g, wg, wu, wd, post_g)


def _proj_conv_kernel(xp_ref, x_ref, xn_ref, g_ref, w_ref, cw_ref, cb_ref, o_ref, *, tiles_per_seq):
    tm = x_ref.shape[0]
    pos = lax.rem(pl.program_id(0), tiles_per_seq)
    keep_prev = (pos != 0).astype(F32)
    keep_next = (pos != tiles_per_seq - 1).astype(F32)
    xa = jnp.concatenate([xp_ref[...] * keep_prev, x_ref[...], xn_ref[...] * keep_next], axis=0)
    p = _dot((_rms(xa) * g_ref[...]).astype(BF16), w_ref[...])
    rows = p.shape[0]
    mid = slice(SUBLANES, SUBLANES + tm)
    prev = pltpu.roll(p, 1, 0)[mid]
    nxt = pltpu.roll(p, rows - 1, 0)[mid]
    cw = cw_ref[...]
    y = cb_ref[...] + prev * cw[0:1] + p[mid] * cw[1:2] + nxt * cw[2:3]
    for part in range(o_ref.shape[0]):
        o_ref[part] = y[:, part * D_HYENA:(part + 1) * D_HYENA].astype(o_ref.dtype)


def _proj_conv(x, layer, g, w_in, conv_w, conv_b, seq):
    n, d = x.shape
    parts = HYENA_ORDER + 1
    c = parts * D_HYENA
    tm = PROJ_TM
    halo_per_tile = tm // SUBLANES
    last_halo = n // SUBLANES - 1
    const2 = lambda i: (0, 0)
    return pl.pallas_call(
        functools.partial(_proj_conv_kernel, tiles_per_seq=seq // tm),
        grid=(n // tm,),
        in_specs=[
            pl.BlockSpec((SUBLANES, d), lambda i: (jnp.maximum(i * halo_per_tile - 1, 0), 0)),
            pl.BlockSpec((tm, d), lambda i: (i, 0)),
            pl.BlockSpec((SUBLANES, d), lambda i: (jnp.minimum((i + 1) * halo_per_tile, last_halo), 0)),
            pl.BlockSpec((1, d), const2),
            _resident((None, d, c), lambda i: (layer, 0, 0)),
            pl.BlockSpec((SHORT_CONV, c), const2),
            pl.BlockSpec((1, c), const2),
        ],
        out_specs=pl.BlockSpec((parts, tm, D_HYENA), lambda i: (0, i, 0)),
        out_shape=jax.ShapeDtypeStruct((parts, n, D_HYENA), BF16),
        compiler_params=_params("parallel"),
        name="hyena_proj_conv",
    )(x, x, x, g, w_in, conv_w, conv_b)


def _filter_kernel(z_ref, w1_ref, b_ref, freq_ref, wi_ref, wo_ref, delta_ref, o_ref):
    hp = functools.partial(jnp.dot, preferred_element_type=F32, precision=lax.Precision.HIGHEST)
    z = z_ref[...]
    b = b_ref[...]
    freq = freq_ref[...]
    h = jnp.sin(freq[0:1] * (hp(z, w1_ref[...]) + b[0:1]))
    for j in range(FILTER_INNER):
        h = jnp.sin(freq[j + 1:j + 2] * (hp(h, wi_ref[j]) + b[j + 1:j + 2]))
    h = hp(h, wo_ref[...])
    t = z[:, 0:1]
    decay = jnp.exp(-t * delta_ref[...])
    tm = z.shape[0]
    row = pl.program_id(0) * tm + lax.broadcasted_iota(jnp.int32, (tm, D_HYENA), 0)
    for blk in range(HYENA_ORDER * N_DIRS):
        hb = h[:, blk * D_HYENA:(blk + 1) * D_HYENA] * decay
        if blk % N_DIRS == 1:
            hb = jnp.where(row == 0, 0.0, hb)
        o_ref[blk] = hb.astype(o_ref.dtype)


def _filter_taps(z, w1, b, freq, w_inner, w_out, abs_delta):
    seq = z.shape[0]
    nsig = HYENA_ORDER * N_DIRS
    tm = FILT_TM
    full = lambda a: pl.BlockSpec(a.shape, lambda i: (0,) * a.ndim)
    return pl.pallas_call(
        _filter_kernel,
        grid=(seq // tm,),
        in_specs=[pl.BlockSpec((tm, z.shape[1]), lambda i: (i, 0)),
                  full(w1), full(b), full(freq), full(w_inner), full(w_out), full(abs_delta)],
        out_specs=pl.BlockSpec((nsig, tm, D_HYENA), lambda i: (0, i, 0)),
        out_shape=jax.ShapeDtypeStruct((nsig, seq, D_HYENA), BF16),
        compiler_params=_params("parallel"),
        name="filter_taps",
    )(z, w1, b, freq, w_inner, w_out, abs_delta)


def _stage1_kernel(g_ref, x_ref, o_ref):
    h1, r, c = x_ref.shape[1:]
    x = x_ref[0].reshape(h1 * r, c)
    o_ref[0] = _dot(g_ref[...], x).astype(o_ref.dtype).reshape(o_ref.shape[1:])


def _stage1(g1, x, part):
    _, nsig, h1, n2, c = x.shape
    r = FFT_ROWS
    n1 = g1.shape[0] // r
    return pl.pallas_call(
        _stage1_kernel,
        grid=(nsig, n2 // r),
        in_specs=[
            pl.BlockSpec(g1.shape, lambda s, j: (0, 0)),
            pl.BlockSpec((None, 1, h1, r, c), lambda s, j: (part, s, 0, j, 0)),
        ],
        out_specs=pl.BlockSpec((1, n1, r, c), lambda s, j: (s, 0, j, 0)),
        out_shape=jax.ShapeDtypeStruct((nsig, n1, n2, c), BF16),
        compiler_params=_params("parallel", "parallel"),
        name="fft_stage1",
    )(g1, x)


def _filter_spectrum_kernel(t_ref, a_ref, k_ref, *, scale):
    t = t_ref[0]
    half = t.shape[0] // 2
    for o in range(HYENA_ORDER):
        xf = _dot(t, a_ref[N_DIRS * o, 0])
        xb = _dot(t, a_ref[N_DIRS * o + 1, 0])
        k_ref[o, 0, :half, :] = (xf[:half] + xb[:half]) * scale
        k_ref[o, 0, half:, :] = (xf[half:] - xb[half:]) * scale


def _filter_spectrum(t2, a, seq):
    nsig, h1, rows, c = a.shape
    return pl.pallas_call(
        functools.partial(_filter_spectrum_kernel, scale=1.0 / seq),
        grid=(h1,),
        in_specs=[
            pl.BlockSpec((1, rows, rows), lambda k: (k, 0, 0)),
            pl.BlockSpec((nsig, 1, rows, c), lambda k: (0, k, 0, 0)),
        ],
        out_specs=pl.BlockSpec((HYENA_ORDER, 1, rows, c), lambda k: (0, k, 0, 0)),
        out_shape=jax.ShapeDtypeStruct((HYENA_ORDER, h1, rows, c), F32),
        compiler_params=_params("parallel"),
        name="filter_spectrum",
    )(t2, a)


def _spectral_kernel(t_ref, tt_ref, a_ref, k_ref, o_ref):
    t = t_ref[0]
    tt = tt_ref[0]
    k = k_ref[0, 0]
    half = k.shape[0] // 2
    kr, ki = k[:half], k[half:]
    for b in range(a_ref.shape[0]):
        x = _dot(t, a_ref[b, 0])
        xr, xi = x[:half], x[half:]
        z = jnp.concatenate([xr * kr - xi * ki, xr * ki + xi * kr], axis=0).astype(BF16)
        o_ref[b, 0] = _dot(tt, z).astype(o_ref.dtype)


def _spectral(t2, t2t, a, k, order):
    bsz, h1, rows, c = a.shape
    return pl.pallas_call(
        _spectral_kernel,
        grid=(h1,),
        in_specs=[
            pl.BlockSpec((1, rows, rows), lambda q: (q, 0, 0)),
            pl.BlockSpec((1, rows, rows), lambda q: (q, 0, 0)),
            pl.BlockSpec((bsz, 1, rows, c), lambda q: (0, q, 0, 0)),
            pl.BlockSpec((1, 1, rows, c), lambda q: (order, q, 0, 0)),
        ],
        out_specs=pl.BlockSpec((bsz, 1, rows, c), lambda q: (0, q, 0, 0)),
        out_shape=jax.ShapeDtypeStruct((bsz, h1, rows, c), BF16),
        compiler_params=_params("parallel"),
        name="hyena_spectral",
    )(t2, t2t, a, k)


def _inverse_gate_kernel(g_ref, b_ref, gate_ref, y_ref, skip_ref, o_ref):
    n1, r, c = b_ref.shape[1:]
    conv = _dot(g_ref[...], b_ref[0].reshape(n1 * r, c))
    y = y_ref[0].reshape(conv.shape).astype(F32)
    gate = gate_ref[0].reshape(conv.shape).astype(F32)
    out = gate * (conv + y * skip_ref[...])
    o_ref[0, 0] = out.astype(o_ref.dtype).reshape(o_ref.shape[2:])


def _inverse_gate(g1t, b, gate, gate_part, y, y_part, skip):
    bsz, n1, n2, c = b.shape
    r = FFT_ROWS
    h1 = g1t.shape[0] // r
    return pl.pallas_call(
        _inverse_gate_kernel,
        grid=(bsz, n2 // r),
        in_specs=[
            pl.BlockSpec(g1t.shape, lambda s, j: (0, 0)),
            pl.BlockSpec((1, n1, r, c), lambda s, j: (s, 0, j, 0)),
            pl.BlockSpec((None, 1, h1, r, c), lambda s, j: (gate_part, s, 0, j, 0)),
            pl.BlockSpec((None, 1, h1, r, c), lambda s, j: (y_part, s, 0, j, 0)),
            pl.BlockSpec((1, c), lambda s, j: (0, 0)),
        ],
        out_specs=pl.BlockSpec((1, 1, h1, r, c), lambda s, j: (0, s, 0, j, 0)),
        out_shape=jax.ShapeDtypeStruct((1, bsz, h1, n2, c), BF16),
        compiler_params=_params("parallel", "parallel"),
        name="hyena_inverse_gate",
    )(g1t, b, gate, y, skip)


def _sgu_kernel(x_ref, g_ref, wu_ref, wv_ref, wg0_ref, wg1_ref, lng_ref, lnb_ref, ws_ref, bs_ref,
                pb_ref, o_ref, s_ref):
    xn = (_rms(x_ref[...]) * g_ref[...]).astype(BF16)
    u = jax.nn.gelu(_dot(xn, wu_ref[...]), approximate=True)
    v = jax.nn.gelu(_dot(xn, wv_ref[...]), approximate=True)
    mu = jnp.mean(v, axis=-1, keepdims=True)
    vc = v - mu
    var = jnp.mean(vc * vc, axis=-1, keepdims=True)
    v = (vc * lax.rsqrt(var + LN_EPS) * lng_ref[...] + lnb_ref[...]).astype(BF16)
    dg = D_SGU // SGU_GROUPS
    for c in range(x_ref.shape[0] // SGU_CHUNK):
        rows = slice(c * SGU_CHUNK, (c + 1) * SGU_CHUNK)
        for grp in range(SGU_GROUPS):
            cols = slice(grp * dg, (grp + 1) * dg)
            s_ref[rows, cols] = _dot(ws_ref[grp], v[rows, cols]) + bs_ref[grp]
    gated = (u * s_ref[...]).astype(BF16)
    branch = _dot(gated, pb_ref[...])
    for h, wg_ref in enumerate((wg0_ref, wg1_ref)):
        cols = slice(h * W_IN_BLOCK, (h + 1) * W_IN_BLOCK)
        gate = jax.nn.sigmoid(_dot(xn, wg_ref[...]))
        o_ref[:, cols] = (gate * branch[:, cols]).astype(o_ref.dtype)


def _sgu(x, layer, g, w_in, ln_g, ln_b, w_s, b_s, p_b):
    n, d = x.shape
    tm = MIX_TM
    const2 = lambda i: (0, 0)
    first = (HYENA_ORDER + 1) * D_HYENA // W_IN_BLOCK
    gate_b = first + 2 * D_SGU // W_IN_BLOCK + d // W_IN_BLOCK
    w_blk = lambda blk: _resident((None, d, W_IN_BLOCK), lambda i: (layer, 0, blk))
    return pl.pallas_call(
        _sgu_kernel,
        grid=(n // tm,),
        in_specs=[
            pl.BlockSpec((tm, d), lambda i: (i, 0)),
            pl.BlockSpec((1, d), const2),
            w_blk(first), w_blk(first + 1), w_blk(gate_b), w_blk(gate_b + 1),
            pl.BlockSpec(ln_g.shape, const2),
            pl.BlockSpec(ln_b.shape, const2),
            _resident((None,) + w_s.shape[1:], lambda i: (layer, 0, 0, 0)),
            _resident(b_s.shape, lambda i: (0, 0, 0)),
            _resident((None,) + p_b.shape[1:], lambda i: (layer, 0, 0)),
        ],
        out_specs=pl.BlockSpec((tm, d), lambda i: (i, 0)),
        out_shape=jax.ShapeDtypeStruct((n, d), BF16),
        scratch_shapes=[pltpu.VMEM((tm, D_SGU), F32)],
        compiler_params=_params("parallel"),
        name="sgu_branch",
    )(x, g, w_in, w_in, w_in, w_in, ln_g, ln_b, w_s, b_s, p_b)


def _merge_kernel(x_ref, g_ref, wg0_ref, wg1_ref, a_ref, pa_ref, mb_ref, wout_ref, post_g_ref, o_ref):
    x = x_ref[...]
    xn = (_rms(x) * g_ref[...]).astype(BF16)
    pa = _dot(a_ref[...], pa_ref[...])
    halves = []
    for h, wg_ref in enumerate((wg0_ref, wg1_ref)):
        cols = slice(h * W_IN_BLOCK, (h + 1) * W_IN_BLOCK)
        gate = jax.nn.sigmoid(_dot(xn, wg_ref[...]))
        halves.append((gate * pa[:, cols] + mb_ref[:, cols].astype(F32)).astype(BF16))
    y = _dot(jnp.concatenate(halves, axis=1), wout_ref[...])
    o_ref[...] = x + _rms(y) * post_g_ref[...]


def _merge(x, layer, g, w_in, a, p_a, mb, w_out, post_g):
    n, d = x.shape
    tm = MIX_TM
    const2 = lambda i: (0, 0)
    gate_a = ((HYENA_ORDER + 1) * D_HYENA + 2 * D_SGU) // W_IN_BLOCK
    w_blk = lambda blk: _resident((None, d, W_IN_BLOCK), lambda i: (layer, 0, blk))
    return pl.pallas_call(
        _merge_kernel,
        grid=(n // tm,),
        in_specs=[
            pl.BlockSpec((tm, d), lambda i: (i, 0)),
            pl.BlockSpec((1, d), const2),
            w_blk(gate_a), w_blk(gate_a + 1),
            pl.BlockSpec((tm, a.shape[1]), lambda i: (i, 0)),
            _resident((None,) + p_a.shape[1:], lambda i: (layer, 0, 0)),
            pl.BlockSpec((tm, d), lambda i: (i, 0)),
            _resident((None,) + w_out.shape[1:], lambda i: (layer, 0, 0)),
            pl.BlockSpec((1, d), const2),
        ],
        out_specs=pl.BlockSpec((tm, d), lambda i: (i, 0)),
        out_shape=jax.ShapeDtypeStruct((n, d), F32),
        compiler_params=_params("parallel"),
        name="merge",
    )(x, g, w_in, w_in, a, p_a, mb, w_out, post_g)


def _filter_positions(seq):
    t = jnp.linspace(0.0, 1.0, seq, dtype=F32)[:, None]
    w = (2.0 * math.pi / seq) * jnp.arange(seq, dtype=F32)[:, None]
    f = jnp.linspace(1e-4, FILTER_BANDS - 1, FILTER_BANDS, dtype=F32)[None, :]
    z = jnp.concatenate([t, jnp.cos(f * w), -jnp.sin(f * w)], axis=-1)
    return jnp.pad(z, ((0, 0), (0, LANES - FILTER_EMB)))


def _decay_rates():
    max_decay = math.log(DECAY_TARGET) / DECAY_FAST_PCT
    min_decay = math.log(DECAY_TARGET) / DECAY_SLOW_PCT
    return jnp.abs(jnp.linspace(min_decay, max_decay, D_HYENA, dtype=F32))[None, :]


def _fft_tables(seq):
    n = 2 * seq
    n2 = FFT_N2
    n1 = n // n2
    h1 = n1 // 2
    k1 = jnp.arange(h1, dtype=jnp.int32)
    m1 = ((2 * k1[:, None] + 1) * k1[None, :]) % (2 * n1)
    a1 = m1.astype(F32) * (math.pi / n1)
    f1 = jnp.stack([jnp.cos(a1), -jnp.sin(a1)], axis=1).reshape(n1, h1)
    q = jnp.arange(n2, dtype=jnp.int32)
    kk = 2 * (k1[:, None, None] + n1 * q[None, :, None]) + 1
    a2 = ((kk * q[None, None, :]) % (2 * n)).astype(F32) * (math.pi / n)
    c, s = jnp.cos(a2), jnp.sin(a2)
    t2 = jnp.concatenate([jnp.concatenate([c, s], axis=2), jnp.concatenate([-s, c], axis=2)], axis=1)
    eye = jnp.eye(FFT_ROWS, dtype=F32)
    g1, g1t = jnp.kron(f1, eye), jnp.kron(f1.T, eye)
    return (g1.astype(BF16), g1t.astype(BF16), t2.astype(BF16), jnp.swapaxes(t2, 1, 2).astype(BF16))


def kernel(x, ffn1_pre_g, ffn1_w_gate, ffn1_w_up, ffn1_w_down, ffn1_post_g, mix_pre_g, w_in, hy_conv_w, hy_conv_b, filt_w1, filt_b, filt_freq, filt_w_inner, filt_w_out, hy_skip, sgu_ln_g, sgu_ln_b, sgu_w_s, sgu_b_s, p_a, p_b, w_out, mix_post_g, ffn2_pre_g, ffn2_w_gate, ffn2_w_up, ffn2_w_down, ffn2_post_g):
    bsz, seq, d = x.shape
    depth = w_in.shape[0]
    n_tok = bsz * seq

    z = _filter_positions(seq)
    abs_delta = _decay_rates()
    g1, g1t, t2, t2t = _fft_tables(seq)
    n2 = FFT_N2
    h1 = seq // n2
    n1 = 2 * h1
    nsig = HYENA_ORDER * N_DIRS
    row = lambda v: v[None, :]
    bf = lambda w: w.astype(BF16)

    ffn1 = (bf(ffn1_w_gate), bf(ffn1_w_up), bf(ffn1_w_down))
    ffn2 = (bf(ffn2_w_gate), bf(ffn2_w_up), bf(ffn2_w_down))
    w_in, p_a, p_b, w_out, sgu_w_s = bf(w_in), bf(p_a), bf(p_b), bf(w_out), bf(sgu_w_s)

    xs = x.reshape(n_tok, d)
    for i in range(depth):
        xs = _ffn(xs, i, row(ffn1_pre_g[i]), *ffn1, row(ffn1_post_g[i]))
        pre_g = row(mix_pre_g[i])

        u = _proj_conv(xs, i, pre_g, w_in, hy_conv_w[i], row(hy_conv_b[i]), seq)
        u = u.reshape(HYENA_ORDER + 1, bsz, h1, n2, D_HYENA)
        w1 = jnp.pad(filt_w1[i], ((0, LANES - FILTER_EMB), (0, 0)))
        taps = _filter_taps(z, w1, filt_b[i], filt_freq[i], filt_w_inner[i], filt_w_out[i], abs_delta)
        taps_a = _stage1(g1, taps.reshape(1, nsig, h1, n2, D_HYENA), 0)
        k_spec = _filter_spectrum(t2, taps_a.reshape(nsig, h1, 2 * n2, D_HYENA), seq)
        y, y_part = u, 0
        for o in range(HYENA_ORDER):
            a1 = _stage1(g1, y, y_part).reshape(bsz, h1, 2 * n2, D_HYENA)
            b2 = _spectral(t2, t2t, a1, k_spec, o).reshape(bsz, n1, n2, D_HYENA)
            y = _inverse_gate(g1t, b2, u, o + 1, y, y_part, row(hy_skip[i, o]))
            y_part = 0
        a = y.reshape(n_tok, D_HYENA)

        b_s = jnp.broadcast_to(sgu_b_s[i][:, :, None], (SGU_GROUPS, SGU_CHUNK, D_SGU // SGU_GROUPS))
        mb = _sgu(xs, i, pre_g, w_in, row(sgu_ln_g[i]), row(sgu_ln_b[i]), sgu_w_s, b_s, p_b)

        xs = _merge(xs, i, pre_g, w_in, a, p_a, mb, w_out, row(mix_post_g[i]))
        xs = _ffn(xs, i, row(ffn2_pre_g[i]), *ffn2, row(ffn2_post_g[i]))
    return xs.reshape(bsz, seq, d)
```

```python
import functools
import math

import jax
import jax.numpy as jnp
from jax import lax
from jax.experimental import pallas as pl
from jax.experimental.pallas import tpu as pltpu

F32 = jnp.float32
BF16 = jnp.bfloat16

D_MODEL = 2048
D_FF = 5632
D_HYENA = D_MODEL // 2
HYENA_ORDER = 2
SHORT_CONV = 3
FILTER_BANDS = 16
FILTER_EMB = 1 + 2 * FILTER_BANDS
FILTER_HIDDEN = 64
FILTER_INNER = 2
DECAY_FAST_PCT = 0.3
DECAY_SLOW_PCT = 1.5
DECAY_TARGET = 1e-2
N_DIRS = 2
D_SGU = D_MODEL // 2
SGU_CHUNK = 128
SGU_GROUPS = 8
NORM_EPS = 1e-6
LN_EPS = 1e-5

LANES = 128
SUBLANES = 8
VMEM_LIMIT_BYTES = 56 * 1024 * 1024

FFN_TM = 512
FFN_TF = 512
PROJ_TM = 512
MIX_TM = 512
FILT_TM = 512
FFT_N2 = 128
FFT_ROWS = 16
ROW_CHUNK = 16
ROW_UNROLL = True
W_IN_BLOCK = 1024


def _params(*sem):
    return pltpu.CompilerParams(dimension_semantics=sem, vmem_limit_bytes=VMEM_LIMIT_BYTES)


def _resident(shape, index_map):
    return pl.BlockSpec(shape, index_map, pipeline_mode=pl.Buffered(1))


def _rms(x):
    return x * lax.rsqrt(jnp.mean(x * x, axis=-1, keepdims=True) + NORM_EPS)


def _dot(a, b):
    return jnp.dot(a, b, preferred_element_type=F32)


def _row_chunks(n_rows, body, unroll=ROW_UNROLL):
    def step(r, carry):
        body(pl.ds(pl.multiple_of(r * ROW_CHUNK, ROW_CHUNK), ROW_CHUNK))
        return carry
    lax.fori_loop(0, n_rows // ROW_CHUNK, step, 0, unroll=unroll)


def _rms_scales(src_ref, scale_ref, zero_ref=None):
    def body(rows):
        a = src_ref[rows, :]
        scale_ref[rows, :] = lax.rsqrt(jnp.mean(a * a, axis=-1, keepdims=True) + NORM_EPS)
        if zero_ref is not None:
            zero_ref[rows, :] = jnp.zeros((ROW_CHUNK, zero_ref.shape[1]), zero_ref.dtype)
    _row_chunks(src_ref.shape[0], body)


def _ffn_kernel(x_ref, pre_g_ref, wg_ref, wu_ref, wd_ref, post_g_ref, o_ref, xn_ref, acc_ref, scale_ref):
    j = pl.program_id(1)
    tm = x_ref.shape[0]

    @pl.when(j == 0)
    def _():
        _rms_scales(x_ref, scale_ref, zero_ref=acc_ref)
        gain = pre_g_ref[...]

        def body(rows):
            xn_ref[rows, :] = (x_ref[rows, :] * scale_ref[rows, :] * gain).astype(BF16)
        _row_chunks(tm, body)

    xn = xn_ref[...]
    g = _dot(xn, wg_ref[...])
    u = _dot(xn, wu_ref[...])
    h = (g * jax.nn.sigmoid(g) * u).astype(BF16)
    acc_ref[...] += _dot(h, wd_ref[...])

    @pl.when(j == pl.num_programs(1) - 1)
    def _():
        _rms_scales(acc_ref, scale_ref)
        half_gain = 0.5 * post_g_ref[...]

        def body(rows):
            o_ref[rows, :] = x_ref[rows, :] + acc_ref[rows, :] * scale_ref[rows, :] * half_gain
        _row_chunks(tm, body)


def _ffn(x, layer, pre_g, wg, wu, wd, post_g):
    n, d = x.shape
    f = wg.shape[2]
    tm, tf = FFN_TM, FFN_TF
    return pl.pallas_call(
        _ffn_kernel,
        grid=(n // tm, f // tf),
        in_specs=[
            pl.BlockSpec((tm, d), lambda i, j: (i, 0)),
            pl.BlockSpec((1, d), lambda i, j: (0, 0)),
            pl.BlockSpec((None, d, tf), lambda i, j: (layer, 0, j)),
            pl.BlockSpec((None, d, tf), lambda i, j: (layer, 0, j)),
            pl.BlockSpec((None, tf, d), lambda i, j: (layer, j, 0)),
            pl.BlockSpec((1, d), lambda i, j: (0, 0)),
        ],
        out_specs=pl.BlockSpec((tm, d), lambda i, j: (i, 0)),
        out_shape=jax.ShapeDtypeStruct((n, d), F32),
        scratch_shapes=[pltpu.VMEM((tm, d), BF16), pltpu.VMEM((tm, d), F32), pltpu.VMEM((tm, 1), F32)],
        compiler_params=_params("parallel", "arbitrary"),
        name="ffn",
    )(x, pre_g, wg, wu, wd, post_g)


def _proj_conv_kernel(xp_ref, x_ref, xn_ref, g_ref, w_ref, cw_ref, cb_ref, o_ref, *, tiles_per_seq):
    tm = x_ref.shape[0]
    pos = lax.rem(pl.program_id(0), tiles_per_seq)
    keep_prev = (pos != 0).astype(F32)
    keep_next = (pos != tiles_per_seq - 1).astype(F32)
    xa = jnp.concatenate([xp_ref[...] * keep_prev, x_ref[...], xn_ref[...] * keep_next], axis=0)
    p = _dot((_rms(xa) * g_ref[...]).astype(BF16), w_ref[...])
    rows = p.shape[0]
    mid = slice(SUBLANES, SUBLANES + tm)
    prev = pltpu.roll(p, 1, 0)[mid]
    nxt = pltpu.roll(p, rows - 1, 0)[mid]
    cw = cw_ref[...]
    y = cb_ref[...] + prev * cw[0:1] + p[mid] * cw[1:2] + nxt * cw[2:3]
    for part in range(o_ref.shape[0]):
        o_ref[part] = y[:, part * D_HYENA:(part + 1) * D_HYENA].astype(o_ref.dtype)


def _proj_conv(x, layer, g, w_in, conv_w, conv_b, seq):
    n, d = x.shape
    parts = HYENA_ORDER + 1
    c = parts * D_HYENA
    tm = PROJ_TM
    halo_per_tile = tm // SUBLANES
    last_halo = n // SUBLANES - 1
    const2 = lambda i: (0, 0)
    return pl.pallas_call(
        functools.partial(_proj_conv_kernel, tiles_per_seq=seq // tm),
        grid=(n // tm,),
        in_specs=[
            pl.BlockSpec((SUBLANES, d), lambda i: (jnp.maximum(i * halo_per_tile - 1, 0), 0)),
            pl.BlockSpec((tm, d), lambda i: (i, 0)),
            pl.BlockSpec((SUBLANES, d), lambda i: (jnp.minimum((i + 1) * halo_per_tile, last_halo), 0)),
            pl.BlockSpec((1, d), const2),
            _resident((None, d, c), lambda i: (layer, 0, 0)),
            pl.BlockSpec((SHORT_CONV, c), const2),
            pl.BlockSpec((1, c), const2),
        ],
        out_specs=pl.BlockSpec((parts, tm, D_HYENA), lambda i: (0, i, 0)),
        out_shape=jax.ShapeDtypeStruct((parts, n, D_HYENA), BF16),
        compiler_params=_params("parallel"),
        name="hyena_proj_conv",
    )(x, x, x, g, w_in, conv_w, conv_b)


def _filter_kernel(z_ref, w1_ref, b_ref, freq_ref, wi_ref, wo_ref, delta_ref, o_ref):
    hp = functools.partial(jnp.dot, preferred_element_type=F32, precision=lax.Precision.HIGHEST)
    z = z_ref[...]
    b = b_ref[...]
    freq = freq_ref[...]
    h = jnp.sin(freq[0:1] * (hp(z, w1_ref[...]) + b[0:1]))
    for j in range(FILTER_INNER):
        h = jnp.sin(freq[j + 1:j + 2] * (hp(h, wi_ref[j]) + b[j + 1:j + 2]))
    h = _dot(h.astype(BF16), wo_ref[...].astype(BF16))
    t = z[:, 0:1]
    decay = jnp.exp(-t * delta_ref[...])
    tm = z.shape[0]
    row = pl.program_id(0) * tm + lax.broadcasted_iota(jnp.int32, (tm, D_HYENA), 0)
    for blk in range(HYENA_ORDER * N_DIRS):
        hb = h[:, blk * D_HYENA:(blk + 1) * D_HYENA] * decay
        if blk % N_DIRS == 1:
            hb = jnp.where(row == 0, 0.0, hb)
        o_ref[blk] = hb.astype(o_ref.dtype)


def _filter_taps(z, w1, b, freq, w_inner, w_out, abs_delta):
    seq = z.shape[0]
    nsig = HYENA_ORDER * N_DIRS
    tm = FILT_TM
    full = lambda a: pl.BlockSpec(a.shape, lambda i: (0,) * a.ndim)
    return pl.pallas_call(
        _filter_kernel,
        grid=(seq // tm,),
        in_specs=[pl.BlockSpec((tm, z.shape[1]), lambda i: (i, 0)),
                  full(w1), full(b), full(freq), full(w_inner), full(w_out), full(abs_delta)],
        out_specs=pl.BlockSpec((nsig, tm, D_HYENA), lambda i: (0, i, 0)),
        out_shape=jax.ShapeDtypeStruct((nsig, seq, D_HYENA), BF16),
        compiler_params=_params("parallel"),
        name="filter_taps",
    )(z, w1, b, freq, w_inner, w_out, abs_delta)


def _stage1_kernel(g_ref, x_ref, o_ref):
    h1, r, c = x_ref.shape[1:]
    x = x_ref[0].reshape(h1 * r, c)
    o_ref[0] = _dot(g_ref[...], x).astype(o_ref.dtype).reshape(o_ref.shape[1:])


def _stage1(g1, x, part):
    _, nsig, h1, n2, c = x.shape
    r = FFT_ROWS
    n1 = g1.shape[0] // r
    return pl.pallas_call(
        _stage1_kernel,
        grid=(nsig, n2 // r),
        in_specs=[
            pl.BlockSpec(g1.shape, lambda s, j: (0, 0)),
            pl.BlockSpec((None, 1, h1, r, c), lambda s, j: (part, s, 0, j, 0)),
        ],
        out_specs=pl.BlockSpec((1, n1, r, c), lambda s, j: (s, 0, j, 0)),
        out_shape=jax.ShapeDtypeStruct((nsig, n1, n2, c), BF16),
        compiler_params=_params("parallel", "parallel"),
        name="fft_stage1",
    )(g1, x)


def _filter_spectrum_kernel(t_ref, a_ref, k_ref, *, scale):
    t = t_ref[0]
    half = t.shape[0] // 2
    for o in range(HYENA_ORDER):
        xf = _dot(t, a_ref[N_DIRS * o, 0])
        xb = _dot(t, a_ref[N_DIRS * o + 1, 0])
        k_ref[o, 0, :half, :] = (xf[:half] + xb[:half]) * scale
        k_ref[o, 0, half:, :] = (xf[half:] - xb[half:]) * scale


def _filter_spectrum(t2, a, seq):
    nsig, h1, rows, c = a.shape
    return pl.pallas_call(
        functools.partial(_filter_spectrum_kernel, scale=1.0 / seq),
        grid=(h1,),
        in_specs=[
            pl.BlockSpec((1, rows, rows), lambda k: (k, 0, 0)),
            pl.BlockSpec((nsig, 1, rows, c), lambda k: (0, k, 0, 0)),
        ],
        out_specs=pl.BlockSpec((HYENA_ORDER, 1, rows, c), lambda k: (0, k, 0, 0)),
        out_shape=jax.ShapeDtypeStruct((HYENA_ORDER, h1, rows, c), F32),
        compiler_params=_params("parallel"),
        name="filter_spectrum",
    )(t2, a)


def _spectral_kernel(t_ref, tt_ref, a_ref, k_ref, o_ref):
    t = t_ref[0]
    tt = tt_ref[0]
    k = k_ref[0, 0]
    half = k.shape[0] // 2
    kr, ki = k[:half], k[half:]
    for b in range(a_ref.shape[0]):
        x = _dot(t, a_ref[b, 0])
        xr, xi = x[:half], x[half:]
        z = jnp.concatenate([xr * kr - xi * ki, xr * ki + xi * kr], axis=0).astype(BF16)
        o_ref[b, 0] = _dot(tt, z).astype(o_ref.dtype)


def _spectral(t2, t2t, a, k, order):
    bsz, h1, rows, c = a.shape
    return pl.pallas_call(
        _spectral_kernel,
        grid=(h1,),
        in_specs=[
            pl.BlockSpec((1, rows, rows), lambda q: (q, 0, 0)),
            pl.BlockSpec((1, rows, rows), lambda q: (q, 0, 0)),
            pl.BlockSpec((bsz, 1, rows, c), lambda q: (0, q, 0, 0)),
            pl.BlockSpec((1, 1, rows, c), lambda q: (order, q, 0, 0)),
        ],
        out_specs=pl.BlockSpec((bsz, 1, rows, c), lambda q: (0, q, 0, 0)),
        out_shape=jax.ShapeDtypeStruct((bsz, h1, rows, c), BF16),
        compiler_params=_params("parallel"),
        name="hyena_spectral",
    )(t2, t2t, a, k)


def _inverse_gate_kernel(g_ref, b_ref, gate_ref, y_ref, skip_ref, o_ref):
    n1, r, c = b_ref.shape[1:]
    conv = _dot(g_ref[...], b_ref[0].reshape(n1 * r, c))
    y = y_ref[0].reshape(conv.shape).astype(F32)
    gate = gate_ref[0].reshape(conv.shape).astype(F32)
    out = gate * (conv + y * skip_ref[...])
    o_ref[0, 0] = out.astype(o_ref.dtype).reshape(o_ref.shape[2:])


def _inverse_gate(g1t, b, gate, gate_part, y, y_part, skip):
    bsz, n1, n2, c = b.shape
    r = FFT_ROWS
    h1 = g1t.shape[0] // r
    return pl.pallas_call(
        _inverse_gate_kernel,
        grid=(bsz, n2 // r),
        in_specs=[
            pl.BlockSpec(g1t.shape, lambda s, j: (0, 0)),
            pl.BlockSpec((1, n1, r, c), lambda s, j: (s, 0, j, 0)),
            pl.BlockSpec((None, 1, h1, r, c), lambda s, j: (gate_part, s, 0, j, 0)),
            pl.BlockSpec((None, 1, h1, r, c), lambda s, j: (y_part, s, 0, j, 0)),
            pl.BlockSpec((1, c), lambda s, j: (0, 0)),
        ],
        out_specs=pl.BlockSpec((1, 1, h1, r, c), lambda s, j: (0, s, 0, j, 0)),
        out_shape=jax.ShapeDtypeStruct((1, bsz, h1, n2, c), BF16),
        compiler_params=_params("parallel", "parallel"),
        name="hyena_inverse_gate",
    )(g1t, b, gate, y, skip)


def _sgu_kernel(x_ref, g_ref, wu_ref, wv_ref, wg0_ref, wg1_ref, lng_ref, lnb_ref, ws_ref, bs_ref,
                pb_ref, o_ref, s_ref):
    xn = (_rms(x_ref[...]) * g_ref[...]).astype(BF16)
    u = jax.nn.gelu(_dot(xn, wu_ref[...]), approximate=True)
    v = jax.nn.gelu(_dot(xn, wv_ref[...]), approximate=True)
    mu = jnp.mean(v, axis=-1, keepdims=True)
    vc = v - mu
    var = jnp.mean(vc * vc, axis=-1, keepdims=True)
    v = (vc * lax.rsqrt(var + LN_EPS) * lng_ref[...] + lnb_ref[...]).astype(BF16)
    dg = D_SGU // SGU_GROUPS
    for c in range(x_ref.shape[0] // SGU_CHUNK):
        rows = slice(c * SGU_CHUNK, (c + 1) * SGU_CHUNK)
        for grp in range(SGU_GROUPS):
            cols = slice(grp * dg, (grp + 1) * dg)
            s_ref[rows, cols] = _dot(ws_ref[grp], v[rows, cols]) + bs_ref[grp]
    gated = (u * s_ref[...]).astype(BF16)
    branch = _dot(gated, pb_ref[...])
    for h, wg_ref in enumerate((wg0_ref, wg1_ref)):
        cols = slice(h * W_IN_BLOCK, (h + 1) * W_IN_BLOCK)
        gate = jax.nn.sigmoid(_dot(xn, wg_ref[...]))
        o_ref[:, cols] = (gate * branch[:, cols]).astype(o_ref.dtype)


def _sgu(x, layer, g, w_in, ln_g, ln_b, w_s, b_s, p_b):
    n, d = x.shape
    tm = MIX_TM
    const2 = lambda i: (0, 0)
    first = (HYENA_ORDER + 1) * D_HYENA // W_IN_BLOCK
    gate_b = first + 2 * D_SGU // W_IN_BLOCK + d // W_IN_BLOCK
    w_blk = lambda blk: _resident((None, d, W_IN_BLOCK), lambda i: (layer, 0, blk))
    return pl.pallas_call(
        _sgu_kernel,
        grid=(n // tm,),
        in_specs=[
            pl.BlockSpec((tm, d), lambda i: (i, 0)),
            pl.BlockSpec((1, d), const2),
            w_blk(first), w_blk(first + 1), w_blk(gate_b), w_blk(gate_b + 1),
            pl.BlockSpec(ln_g.shape, const2),
            pl.BlockSpec(ln_b.shape, const2),
            _resident((None,) + w_s.shape[1:], lambda i: (layer, 0, 0, 0)),
            _resident(b_s.shape, lambda i: (0, 0, 0)),
            _resident((None,) + p_b.shape[1:], lambda i: (layer, 0, 0)),
        ],
        out_specs=pl.BlockSpec((tm, d), lambda i: (i, 0)),
        out_shape=jax.ShapeDtypeStruct((n, d), BF16),
        scratch_shapes=[pltpu.VMEM((tm, D_SGU), F32)],
        compiler_params=_params("parallel"),
        name="sgu_branch",
    )(x, g, w_in, w_in, w_in, w_in, ln_g, ln_b, w_s, b_s, p_b)


def _merge_kernel(x_ref, g_ref, wg0_ref, wg1_ref, a_ref, pa_ref, mb_ref, wout_ref, post_g_ref, o_ref):
    x = x_ref[...]
    xn = (_rms(x) * g_ref[...]).astype(BF16)
    pa = _dot(a_ref[...], pa_ref[...])
    halves = []
    for h, wg_ref in enumerate((wg0_ref, wg1_ref)):
        cols = slice(h * W_IN_BLOCK, (h + 1) * W_IN_BLOCK)
        gate = jax.nn.sigmoid(_dot(xn, wg_ref[...]))
        halves.append((gate * pa[:, cols] + mb_ref[:, cols].astype(F32)).astype(BF16))
    y = _dot(jnp.concatenate(halves, axis=1), wout_ref[...])
    o_ref[...] = x + _rms(y) * post_g_ref[...]


def _merge(x, layer, g, w_in, a, p_a, mb, w_out, post_g):
    n, d = x.shape
    tm = MIX_TM
    const2 = lambda i: (0, 0)
    gate_a = ((HYENA_ORDER + 1) * D_HYENA + 2 * D_SGU) // W_IN_BLOCK
    w_blk = lambda blk: _resident((None, d, W_IN_BLOCK), lambda i: (layer, 0, blk))
    return pl.pallas_call(
        _merge_kernel,
        grid=(n // tm,),
        in_specs=[
            pl.BlockSpec((tm, d), lambda i: (i, 0)),
            pl.BlockSpec((1, d), const2),
            w_blk(gate_a), w_blk(gate_a + 1),
            pl.BlockSpec((tm, a.shape[1]), lambda i: (i, 0)),
            _resident((None,) + p_a.shape[1:], lambda i: (layer, 0, 0)),
            pl.BlockSpec((tm, d), lambda i: (i, 0)),
            _resident((None,) + w_out.shape[1:], lambda i: (layer, 0, 0)),
            pl.BlockSpec((1, d), const2),
        ],
        out_specs=pl.BlockSpec((tm, d), lambda i: (i, 0)),
        out_shape=jax.ShapeDtypeStruct((n, d), F32),
        compiler_params=_params("parallel"),
        name="merge",
    )(x, g, w_in, w_in, a, p_a, mb, w_out, post_g)


def _filter_positions(seq):
    t = jnp.linspace(0.0, 1.0, seq, dtype=F32)[:, None]
    w = (2.0 * math.pi / seq) * jnp.arange(seq, dtype=F32)[:, None]
    f = jnp.linspace(1e-4, FILTER_BANDS - 1, FILTER_BANDS, dtype=F32)[None, :]
    z = jnp.concatenate([t, jnp.cos(f * w), -jnp.sin(f * w)], axis=-1)
    return jnp.pad(z, ((0, 0), (0, LANES - FILTER_EMB)))


def _decay_rates():
    max_decay = math.log(DECAY_TARGET) / DECAY_FAST_PCT
    min_decay = math.log(DECAY_TARGET) / DECAY_SLOW_PCT
    return jnp.abs(jnp.linspace(min_decay, max_decay, D_HYENA, dtype=F32))[None, :]


def _fft_tables(seq):
    n = 2 * seq
    n2 = FFT_N2
    n1 = n // n2
    h1 = n1 // 2
    k1 = jnp.arange(h1, dtype=jnp.int32)
    m1 = ((2 * k1[:, None] + 1) * k1[None, :]) % (2 * n1)
    a1 = m1.astype(F32) * (math.pi / n1)
    f1 = jnp.stack([jnp.cos(a1), -jnp.sin(a1)], axis=1).reshape(n1, h1)
    q = jnp.arange(n2, dtype=jnp.int32)
    kk = 2 * (k1[:, None, None] + n1 * q[None, :, None]) + 1
    a2 = ((kk * q[None, None, :]) % (2 * n)).astype(F32) * (math.pi / n)
    c, s = jnp.cos(a2), jnp.sin(a2)
    t2 = jnp.concatenate([jnp.concatenate([c, s], axis=2), jnp.concatenate([-s, c], axis=2)], axis=1)
    eye = jnp.eye(FFT_ROWS, dtype=F32)
    g1, g1t = jnp.kron(f1, eye), jnp.kron(f1.T, eye)
    return (g1.astype(BF16), g1t.astype(BF16), t2.astype(BF16), jnp.swapaxes(t2, 1, 2).astype(BF16))


def kernel(x, ffn1_pre_g, ffn1_w_gate, ffn1_w_up, ffn1_w_down, ffn1_post_g, mix_pre_g, w_in, hy_conv_w, hy_conv_b, filt_w1, filt_b, filt_freq, filt_w_inner, filt_w_out, hy_skip, sgu_ln_g, sgu_ln_b, sgu_w_s, sgu_b_s, p_a, p_b, w_out, mix_post_g, ffn2_pre_g, ffn2_w_gate, ffn2_w_up, ffn2_w_down, ffn2_post_g):
    bsz, seq, d = x.shape
    depth = w_in.shape[0]
    n_tok = bsz * seq

    z = _filter_positions(seq)
    abs_delta = _decay_rates()
    g1, g1t, t2, t2t = _fft_tables(seq)
    n2 = FFT_N2
    h1 = seq // n2
    n1 = 2 * h1
    nsig = HYENA_ORDER * N_DIRS
    row = lambda v: v[None, :]
    bf = lambda w: w.astype(BF16)

    ffn1 = (bf(ffn1_w_gate), bf(ffn1_w_up), bf(ffn1_w_down))
    ffn2 = (bf(ffn2_w_gate), bf(ffn2_w_up), bf(ffn2_w_down))
    w_in, p_a, p_b, w_out, sgu_w_s = bf(w_in), bf(p_a), bf(p_b), bf(w_out), bf(sgu_w_s)

    xs = x.reshape(n_tok, d)
    for i in range(depth):
        xs = _ffn(xs, i, row(ffn1_pre_g[i]), *ffn1, row(ffn1_post_g[i]))
        pre_g = row(mix_pre_g[i])

        u = _proj_conv(xs, i, pre_g, w_in, hy_conv_w[i], row(hy_conv_b[i]), seq)
        u = u.reshape(HYENA_ORDER + 1, bsz, h1, n2, D_HYENA)
        w1 = jnp.pad(filt_w1[i], ((0, LANES - FILTER_EMB), (0, 0)))
        taps = _filter_taps(z, w1, filt_b[i], filt_freq[i], filt_w_inner[i], filt_w_out[i], abs_delta)
        taps_a = _stage1(g1, taps.reshape(1, nsig, h1, n2, D_HYENA), 0)
        k_spec = _filter_spectrum(t2, taps_a.reshape(nsig, h1, 2 * n2, D_HYENA), seq)
        y, y_part = u, 0
        for o in range(HYENA_ORDER):
            a1 = _stage1(g1, y, y_part).reshape(bsz, h1, 2 * n2, D_HYENA)
            b2 = _spectral(t2, t2t, a1, k_spec, o).reshape(bsz, n1, n2, D_HYENA)
            y = _inverse_gate(g1t, b2, u, o + 1, y, y_part, row(hy_skip[i, o]))
            y_part = 0
        a = y.reshape(n_tok, D_HYENA)

        b_s = jnp.broadcast_to(sgu_b_s[i][:, :, None], (SGU_GROUPS, SGU_CHUNK, D_SGU // SGU_GROUPS))
        mb = _sgu(xs, i, pre_g, w_in, row(sgu_ln_g[i]), row(sgu_ln_b[i]), sgu_w_s, b_s, p_b)

        xs = _merge(xs, i, pre_g, w_in, a, p_a, mb, w_out, row(mix_post_g[i]))
        xs = _ffn(xs, i, row(ffn2_pre_g[i]), *ffn2, row(ffn2_post_g[i]))
    return xs.reshape(bsz, seq, d)
```

```python
import functools
import math

import jax
import jax.numpy as jnp
from jax import lax
from jax.experimental import pallas as pl
from jax.experimental.pallas import tpu as pltpu

F32 = jnp.float32
BF16 = jnp.bfloat16

D_MODEL = 2048
D_FF = 5632
D_HYENA = D_MODEL // 2
HYENA_ORDER = 2
SHORT_CONV = 3
FILTER_BANDS = 16
FILTER_EMB = 1 + 2 * FILTER_BANDS
FILTER_HIDDEN = 64
FILTER_INNER = 2
DECAY_FAST_PCT = 0.3
DECAY_SLOW_PCT = 1.5
DECAY_TARGET = 1e-2
N_DIRS = 2
D_SGU = D_MODEL // 2
SGU_CHUNK = 128
SGU_GROUPS = 8
NORM_EPS = 1e-6
LN_EPS = 1e-5

LANES = 128
SUBLANES = 8
VMEM_LIMIT_BYTES = 56 * 1024 * 1024

FFN_TM = 512
FFN_TF = 512
PROJ_TM = 512
MIX_TM = 512
FILT_TM = 512
FFT_N2 = 128
FFT_ROWS = 16
FFT_GROUPS = 2
SPECTRAL_K1 = 2
ROW_CHUNK = 16
ROW_UNROLL = True
W_IN_BLOCK = 1024


def _params(*sem):
    return pltpu.CompilerParams(dimension_semantics=sem, vmem_limit_bytes=VMEM_LIMIT_BYTES)


def _resident(shape, index_map):
    return pl.BlockSpec(shape, index_map, pipeline_mode=pl.Buffered(1))


def _rms(x):
    return x * lax.rsqrt(jnp.mean(x * x, axis=-1, keepdims=True) + NORM_EPS)


def _dot(a, b):
    return jnp.dot(a, b, preferred_element_type=F32)


def _row_chunks(n_rows, body, unroll=ROW_UNROLL):
    def step(r, carry):
        body(pl.ds(pl.multiple_of(r * ROW_CHUNK, ROW_CHUNK), ROW_CHUNK))
        return carry
    lax.fori_loop(0, n_rows // ROW_CHUNK, step, 0, unroll=unroll)


def _rms_scales(src_ref, scale_ref, zero_ref=None):
    def body(rows):
        a = src_ref[rows, :]
        scale_ref[rows, :] = lax.rsqrt(jnp.mean(a * a, axis=-1, keepdims=True) + NORM_EPS)
        if zero_ref is not None:
            zero_ref[rows, :] = jnp.zeros((ROW_CHUNK, zero_ref.shape[1]), zero_ref.dtype)
    _row_chunks(src_ref.shape[0], body)


def _ffn_kernel(x_ref, pre_g_ref, wg_ref, wu_ref, wd_ref, post_g_ref, o_ref, xn_ref, acc_ref, scale_ref):
    j = pl.program_id(1)
    tm = x_ref.shape[0]

    @pl.when(j == 0)
    def _():
        _rms_scales(x_ref, scale_ref, zero_ref=acc_ref)
        gain = pre_g_ref[...]

        def body(rows):
            xn_ref[rows, :] = (x_ref[rows, :] * scale_ref[rows, :] * gain).astype(BF16)
        _row_chunks(tm, body)

    xn = xn_ref[...]
    g = _dot(xn, wg_ref[...])
    u = _dot(xn, wu_ref[...])
    h = (g * jax.nn.sigmoid(g) * u).astype(BF16)
    acc_ref[...] += _dot(h, wd_ref[...])

    @pl.when(j == pl.num_programs(1) - 1)
    def _():
        _rms_scales(acc_ref, scale_ref)
        half_gain = 0.5 * post_g_ref[...]

        def body(rows):
            o_ref[rows, :] = x_ref[rows, :] + acc_ref[rows, :] * scale_ref[rows, :] * half_gain
        _row_chunks(tm, body)


def _ffn(x, layer, pre_g, wg, wu, wd, post_g):
    n, d = x.shape
    f = wg.shape[2]
    tm, tf = FFN_TM, FFN_TF
    return pl.pallas_call(
        _ffn_kernel,
        grid=(n // tm, f // tf),
        in_specs=[
            pl.BlockSpec((tm, d), lambda i, j: (i, 0)),
            pl.BlockSpec((1, d), lambda i, j: (0, 0)),
            pl.BlockSpec((None, d, tf), lambda i, j: (layer, 0, j)),
            pl.BlockSpec((None, d, tf), lambda i, j: (layer, 0, j)),
            pl.BlockSpec((None, tf, d), lambda i, j: (layer, j, 0)),
            pl.BlockSpec((1, d), lambda i, j: (0, 0)),
        ],
        out_specs=pl.BlockSpec((tm, d), lambda i, j: (i, 0)),
        out_shape=jax.ShapeDtypeStruct((n, d), F32),
        scratch_shapes=[pltpu.VMEM((tm, d), BF16), pltpu.VMEM((tm, d), F32), pltpu.VMEM((tm, 1), F32)],
        compiler_params=_params("parallel", "arbitrary"),
        name="ffn",
    )(x, pre_g, wg, wu, wd, post_g)


def _proj_conv_kernel(xp_ref, x_ref, xn_ref, g_ref, w_ref, cw_ref, cb_ref, o_ref, *, tiles_per_seq):
    tm = x_ref.shape[0]
    pos = lax.rem(pl.program_id(0), tiles_per_seq)
    keep_prev = (pos != 0).astype(F32)
    keep_next = (pos != tiles_per_seq - 1).astype(F32)
    xa = jnp.concatenate([xp_ref[...] * keep_prev, x_ref[...], xn_ref[...] * keep_next], axis=0)
    xn = (_rms(xa) * g_ref[...]).astype(BF16)
    rows = xn.shape[0]
    mid = slice(SUBLANES, SUBLANES + tm)
    for part in range(o_ref.shape[0]):
        cols = slice(part * D_HYENA, (part + 1) * D_HYENA)
        p = _dot(xn, w_ref[:, cols])
        prev = pltpu.roll(p, 1, 0)[mid]
        nxt = pltpu.roll(p, rows - 1, 0)[mid]
        cw = cw_ref[:, cols]
        y = cb_ref[:, cols] + prev * cw[0:1] + p[mid] * cw[1:2] + nxt * cw[2:3]
        o_ref[part] = y.astype(o_ref.dtype)


def _proj_conv(x, layer, g, w_in, conv_w, conv_b, seq):
    n, d = x.shape
    parts = HYENA_ORDER + 1
    c = parts * D_HYENA
    tm = PROJ_TM
    halo_per_tile = tm // SUBLANES
    last_halo = n // SUBLANES - 1
    const2 = lambda i: (0, 0)
    return pl.pallas_call(
        functools.partial(_proj_conv_kernel, tiles_per_seq=seq // tm),
        grid=(n // tm,),
        in_specs=[
            pl.BlockSpec((SUBLANES, d), lambda i: (jnp.maximum(i * halo_per_tile - 1, 0), 0)),
            pl.BlockSpec((tm, d), lambda i: (i, 0)),
            pl.BlockSpec((SUBLANES, d), lambda i: (jnp.minimum((i + 1) * halo_per_tile, last_halo), 0)),
            pl.BlockSpec((1, d), const2),
            _resident((None, d, c), lambda i: (layer, 0, 0)),
            pl.BlockSpec((SHORT_CONV, c), const2),
            pl.BlockSpec((1, c), const2),
        ],
        out_specs=pl.BlockSpec((parts, tm, D_HYENA), lambda i: (0, i, 0)),
        out_shape=jax.ShapeDtypeStruct((parts, n, D_HYENA), BF16),
        compiler_params=_params("parallel"),
        name="hyena_proj_conv",
    )(x, x, x, g, w_in, conv_w, conv_b)


def _filter_kernel(z_ref, w1_ref, b_ref, freq_ref, wi_ref, wo_ref, delta_ref, o_ref):
    hp = functools.partial(jnp.dot, preferred_element_type=F32, precision=lax.Precision.HIGHEST)
    z = z_ref[...]
    b = b_ref[...]
    freq = freq_ref[...]
    h = jnp.sin(freq[0:1] * (hp(z, w1_ref[...]) + b[0:1]))
    for j in range(FILTER_INNER):
        h = jnp.sin(freq[j + 1:j + 2] * (hp(h, wi_ref[j]) + b[j + 1:j + 2]))
    h = _dot(h.astype(BF16), wo_ref[...].astype(BF16))
    t = z[:, 0:1]
    decay = jnp.exp(-t * delta_ref[...])
    tm = z.shape[0]
    row = pl.program_id(0) * tm + lax.broadcasted_iota(jnp.int32, (tm, D_HYENA), 0)
    for blk in range(HYENA_ORDER * N_DIRS):
        hb = h[:, blk * D_HYENA:(blk + 1) * D_HYENA] * decay
        if blk % N_DIRS == 1:
            hb = jnp.where(row == 0, 0.0, hb)
        o_ref[blk] = hb.astype(o_ref.dtype)


def _filter_taps(z, w1, b, freq, w_inner, w_out, abs_delta):
    seq = z.shape[0]
    nsig = HYENA_ORDER * N_DIRS
    tm = FILT_TM
    full = lambda a: pl.BlockSpec(a.shape, lambda i: (0,) * a.ndim)
    return pl.pallas_call(
        _filter_kernel,
        grid=(seq // tm,),
        in_specs=[pl.BlockSpec((tm, z.shape[1]), lambda i: (i, 0)),
                  full(w1), full(b), full(freq), full(w_inner), full(w_out), full(abs_delta)],
        out_specs=pl.BlockSpec((nsig, tm, D_HYENA), lambda i: (0, i, 0)),
        out_shape=jax.ShapeDtypeStruct((nsig, seq, D_HYENA), BF16),
        compiler_params=_params("parallel"),
        name="filter_taps",
    )(z, w1, b, freq, w_inner, w_out, abs_delta)


def _slab_groups(rows_per_block):
    return [slice(g * FFT_ROWS, (g + 1) * FFT_ROWS) for g in range(rows_per_block // FFT_ROWS)]


def _stage1_kernel(g_ref, x_ref, o_ref):
    h1, rb, c = x_ref.shape[1:]
    n1 = o_ref.shape[1]
    for rows in _slab_groups(rb):
        x = x_ref[0, :, rows, :].reshape(h1 * FFT_ROWS, c)
        o_ref[0, :, rows, :] = _dot(g_ref[...], x).astype(o_ref.dtype).reshape(n1, FFT_ROWS, c)


def _stage1(g1, x, part):
    _, nsig, h1, n2, c = x.shape
    r = FFT_ROWS * FFT_GROUPS
    n1 = g1.shape[0] // FFT_ROWS
    return pl.pallas_call(
        _stage1_kernel,
        grid=(nsig, n2 // r),
        in_specs=[
            pl.BlockSpec(g1.shape, lambda s, j: (0, 0)),
            pl.BlockSpec((None, 1, h1, r, c), lambda s, j: (part, s, 0, j, 0)),
        ],
        out_specs=pl.BlockSpec((1, n1, r, c), lambda s, j: (s, 0, j, 0)),
        out_shape=jax.ShapeDtypeStruct((nsig, n1, n2, c), BF16),
        compiler_params=_params("parallel", "parallel"),
        name="fft_stage1",
    )(g1, x)


def _filter_spectrum_kernel(t_ref, a_ref, k_ref, *, scale):
    t = t_ref[0]
    half = t.shape[0] // 2
    for o in range(HYENA_ORDER):
        xf = _dot(t, a_ref[N_DIRS * o, 0])
        xb = _dot(t, a_ref[N_DIRS * o + 1, 0])
        k_ref[o, 0, :half, :] = (xf[:half] + xb[:half]) * scale
        k_ref[o, 0, half:, :] = (xf[half:] - xb[half:]) * scale


def _filter_spectrum(t2, a, seq):
    nsig, h1, rows, c = a.shape
    return pl.pallas_call(
        functools.partial(_filter_spectrum_kernel, scale=1.0 / seq),
        grid=(h1,),
        in_specs=[
            pl.BlockSpec((1, rows, rows), lambda k: (k, 0, 0)),
            pl.BlockSpec((nsig, 1, rows, c), lambda k: (0, k, 0, 0)),
        ],
        out_specs=pl.BlockSpec((HYENA_ORDER, 1, rows, c), lambda k: (0, k, 0, 0)),
        out_shape=jax.ShapeDtypeStruct((HYENA_ORDER, h1, rows, c), F32),
        compiler_params=_params("parallel"),
        name="filter_spectrum",
    )(t2, a)


def _spectral_kernel(t_ref, tt_ref, a_ref, k_ref, o_ref):
    for q in range(t_ref.shape[0]):
        t = t_ref[q]
        tt = tt_ref[q]
        k = k_ref[0, q]
        half = k.shape[0] // 2
        kr, ki = k[:half], k[half:]
        for b in range(a_ref.shape[0]):
            x = _dot(t, a_ref[b, q])
            xr, xi = x[:half], x[half:]
            z = jnp.concatenate([xr * kr - xi * ki, xr * ki + xi * kr], axis=0).astype(BF16)
            o_ref[b, q] = _dot(tt, z).astype(o_ref.dtype)


def _spectral(t2, t2t, a, k, order):
    bsz, h1, rows, c = a.shape
    kb = SPECTRAL_K1
    return pl.pallas_call(
        _spectral_kernel,
        grid=(h1 // kb,),
        in_specs=[
            pl.BlockSpec((kb, rows, rows), lambda q: (q, 0, 0)),
            pl.BlockSpec((kb, rows, rows), lambda q: (q, 0, 0)),
            pl.BlockSpec((bsz, kb, rows, c), lambda q: (0, q, 0, 0)),
            pl.BlockSpec((1, kb, rows, c), lambda q: (order, q, 0, 0)),
        ],
        out_specs=pl.BlockSpec((bsz, kb, rows, c), lambda q: (0, q, 0, 0)),
        out_shape=jax.ShapeDtypeStruct((bsz, h1, rows, c), BF16),
        compiler_params=_params("parallel"),
        name="hyena_spectral",
    )(t2, t2t, a, k)


def _inverse_gate_kernel(gt_ref, b_ref, gate_ref, y_ref, skip_ref, *rest, emit_stage1):
    if emit_stage1:
        g_ref, o_ref, a_ref = rest
    else:
        (o_ref,), a_ref = rest, None
    n1, rb, c = b_ref.shape[1:]
    h1 = o_ref.shape[2]
    for rows in _slab_groups(rb):
        conv = _dot(gt_ref[...], b_ref[0, :, rows, :].reshape(n1 * FFT_ROWS, c))
        y = y_ref[0, :, rows, :].reshape(conv.shape).astype(F32)
        gate = gate_ref[0, :, rows, :].reshape(conv.shape).astype(F32)
        out = (gate * (conv + y * skip_ref[...])).astype(BF16)
        o_ref[0, 0, :, rows, :] = out.reshape(h1, FFT_ROWS, c)
        if a_ref is not None:
            a_ref[0, :, rows, :] = _dot(g_ref[...], out).astype(BF16).reshape(n1, FFT_ROWS, c)


def _inverse_gate(g1t, b, gate, gate_part, y, y_part, skip, g1=None):
    bsz, n1, n2, c = b.shape
    r = FFT_ROWS * FFT_GROUPS
    h1 = g1t.shape[0] // FFT_ROWS
    const2 = lambda s, j: (0, 0)
    in_specs = [
        pl.BlockSpec(g1t.shape, const2),
        pl.BlockSpec((1, n1, r, c), lambda s, j: (s, 0, j, 0)),
        pl.BlockSpec((None, 1, h1, r, c), lambda s, j: (gate_part, s, 0, j, 0)),
        pl.BlockSpec((None, 1, h1, r, c), lambda s, j: (y_part, s, 0, j, 0)),
        pl.BlockSpec((1, c), const2),
    ]
    out_specs = [pl.BlockSpec((1, 1, h1, r, c), lambda s, j: (0, s, 0, j, 0))]
    out_shape = [jax.ShapeDtypeStruct((1, bsz, h1, n2, c), BF16)]
    args = [g1t, b, gate, y, skip]
    if g1 is not None:
        in_specs.append(pl.BlockSpec(g1.shape, const2))
        out_specs.append(pl.BlockSpec((1, n1, r, c), lambda s, j: (s, 0, j, 0)))
        out_shape.append(jax.ShapeDtypeStruct((bsz, n1, n2, c), BF16))
        args.append(g1)
    return pl.pallas_call(
        functools.partial(_inverse_gate_kernel, emit_stage1=g1 is not None),
        grid=(bsz, n2 // r),
        in_specs=in_specs,
        out_specs=out_specs,
        out_shape=out_shape,
        compiler_params=_params("parallel", "parallel"),
        name="hyena_inverse_gate",
    )(*args)


def _sgu_kernel(x_ref, g_ref, wu_ref, wv_ref, wg0_ref, wg1_ref, lng_ref, lnb_ref, ws_ref, bs_ref,
                pb_ref, o_ref, s_ref):
    xn = (_rms(x_ref[...]) * g_ref[...]).astype(BF16)
    u = jax.nn.gelu(_dot(xn, wu_ref[...]), approximate=True)
    v = jax.nn.gelu(_dot(xn, wv_ref[...]), approximate=True)
    mu = jnp.mean(v, axis=-1, keepdims=True)
    vc = v - mu
    var = jnp.mean(vc * vc, axis=-1, keepdims=True)
    v = (vc * lax.rsqrt(var + LN_EPS) * lng_ref[...] + lnb_ref[...]).astype(BF16)
    dg = D_SGU // SGU_GROUPS
    for c in range(x_ref.shape[0] // SGU_CHUNK):
        rows = slice(c * SGU_CHUNK, (c + 1) * SGU_CHUNK)
        for grp in range(SGU_GROUPS):
            cols = slice(grp * dg, (grp + 1) * dg)
            s_ref[rows, cols] = _dot(ws_ref[grp], v[rows, cols]) + bs_ref[grp]
    gated = (u * s_ref[...]).astype(BF16)
    branch = _dot(gated, pb_ref[...])
    for h, wg_ref in enumerate((wg0_ref, wg1_ref)):
        cols = slice(h * W_IN_BLOCK, (h + 1) * W_IN_BLOCK)
        gate = jax.nn.sigmoid(_dot(xn, wg_ref[...]))
        o_ref[:, cols] = (gate * branch[:, cols]).astype(o_ref.dtype)


def _sgu(x, layer, g, w_in, ln_g, ln_b, w_s, b_s, p_b):
    n, d = x.shape
    tm = MIX_TM
    const2 = lambda i: (0, 0)
    first = (HYENA_ORDER + 1) * D_HYENA // W_IN_BLOCK
    gate_b = first + 2 * D_SGU // W_IN_BLOCK + d // W_IN_BLOCK
    w_blk = lambda blk: _resident((None, d, W_IN_BLOCK), lambda i: (layer, 0, blk))
    return pl.pallas_call(
        _sgu_kernel,
        grid=(n // tm,),
        in_specs=[
            pl.BlockSpec((tm, d), lambda i: (i, 0)),
            pl.BlockSpec((1, d), const2),
            w_blk(first), w_blk(first + 1), w_blk(gate_b), w_blk(gate_b + 1),
            pl.BlockSpec(ln_g.shape, const2),
            pl.BlockSpec(ln_b.shape, const2),
            _resident((None,) + w_s.shape[1:], lambda i: (layer, 0, 0, 0)),
            _resident(b_s.shape, lambda i: (0, 0, 0)),
            _resident((None,) + p_b.shape[1:], lambda i: (layer, 0, 0)),
        ],
        out_specs=pl.BlockSpec((tm, d), lambda i: (i, 0)),
        out_shape=jax.ShapeDtypeStruct((n, d), BF16),
        scratch_shapes=[pltpu.VMEM((tm, D_SGU), F32)],
        compiler_params=_params("parallel"),
        name="sgu_branch",
    )(x, g, w_in, w_in, w_in, w_in, ln_g, ln_b, w_s, b_s, p_b)


def _merge_kernel(x_ref, g_ref, wg0_ref, wg1_ref, a_ref, pa_ref, mb_ref, wout_ref, post_g_ref, o_ref):
    x = x_ref[...]
    xn = (_rms(x) * g_ref[...]).astype(BF16)
    pa = _dot(a_ref[...], pa_ref[...])
    halves = []
    for h, wg_ref in enumerate((wg0_ref, wg1_ref)):
        cols = slice(h * W_IN_BLOCK, (h + 1) * W_IN_BLOCK)
        gate = jax.nn.sigmoid(_dot(xn, wg_ref[...]))
        halves.append((gate * pa[:, cols] + mb_ref[:, cols].astype(F32)).astype(BF16))
    y = _dot(jnp.concatenate(halves, axis=1), wout_ref[...])
    o_ref[...] = x + _rms(y) * post_g_ref[...]


def _merge(x, layer, g, w_in, a, p_a, mb, w_out, post_g):
    n, d = x.shape
    tm = MIX_TM
    const2 = lambda i: (0, 0)
    gate_a = ((HYENA_ORDER + 1) * D_HYENA + 2 * D_SGU) // W_IN_BLOCK
    w_blk = lambda blk: _resident((None, d, W_IN_BLOCK), lambda i: (layer, 0, blk))
    return pl.pallas_call(
        _merge_kernel,
        grid=(n // tm,),
        in_specs=[
            pl.BlockSpec((tm, d), lambda i: (i, 0)),
            pl.BlockSpec((1, d), const2),
            w_blk(gate_a), w_blk(gate_a + 1),
            pl.BlockSpec((tm, a.shape[1]), lambda i: (i, 0)),
            _resident((None,) + p_a.shape[1:], lambda i: (layer, 0, 0)),
            pl.BlockSpec((tm, d), lambda i: (i, 0)),
            _resident((None,) + w_out.shape[1:], lambda i: (layer, 0, 0)),
            pl.BlockSpec((1, d), const2),
        ],
        out_specs=pl.BlockSpec((tm, d), lambda i: (i, 0)),
        out_shape=jax.ShapeDtypeStruct((n, d), F32),
        compiler_params=_params("parallel"),
        name="merge",
    )(x, g, w_in, w_in, a, p_a, mb, w_out, post_g)


def _filter_positions(seq):
    t = jnp.linspace(0.0, 1.0, seq, dtype=F32)[:, None]
    w = (2.0 * math.pi / seq) * jnp.arange(seq, dtype=F32)[:, None]
    f = jnp.linspace(1e-4, FILTER_BANDS - 1, FILTER_BANDS, dtype=F32)[None, :]
    z = jnp.concatenate([t, jnp.cos(f * w), -jnp.sin(f * w)], axis=-1)
    return jnp.pad(z, ((0, 0), (0, LANES - FILTER_EMB)))


def _decay_rates():
    max_decay = math.log(DECAY_TARGET) / DECAY_FAST_PCT
    min_decay = math.log(DECAY_TARGET) / DECAY_SLOW_PCT
    return jnp.abs(jnp.linspace(min_decay, max_decay, D_HYENA, dtype=F32))[None, :]


def _fft_tables(seq):
    n = 2 * seq
    n2 = FFT_N2
    n1 = n // n2
    h1 = n1 // 2
    k1 = jnp.arange(h1, dtype=jnp.int32)
    m1 = ((2 * k1[:, None] + 1) * k1[None, :]) % (2 * n1)
    a1 = m1.astype(F32) * (math.pi / n1)
    f1 = jnp.stack([jnp.cos(a1), -jnp.sin(a1)], axis=1).reshape(n1, h1)
    q = jnp.arange(n2, dtype=jnp.int32)
    kk = 2 * (k1[:, None, None] + n1 * q[None, :, None]) + 1
    a2 = ((kk * q[None, None, :]) % (2 * n)).astype(F32) * (math.pi / n)
    c, s = jnp.cos(a2), jnp.sin(a2)
    t2 = jnp.concatenate([jnp.concatenate([c, s], axis=2), jnp.concatenate([-s, c], axis=2)], axis=1)
    eye = jnp.eye(FFT_ROWS, dtype=F32)
    g1, g1t = jnp.kron(f1, eye), jnp.kron(f1.T, eye)
    return (g1.astype(BF16), g1t.astype(BF16), t2.astype(BF16), jnp.swapaxes(t2, 1, 2).astype(BF16))


def kernel(x, ffn1_pre_g, ffn1_w_gate, ffn1_w_up, ffn1_w_down, ffn1_post_g, mix_pre_g, w_in, hy_conv_w, hy_conv_b, filt_w1, filt_b, filt_freq, filt_w_inner, filt_w_out, hy_skip, sgu_ln_g, sgu_ln_b, sgu_w_s, sgu_b_s, p_a, p_b, w_out, mix_post_g, ffn2_pre_g, ffn2_w_gate, ffn2_w_up, ffn2_w_down, ffn2_post_g):
    bsz, seq, d = x.shape
    depth = w_in.shape[0]
    n_tok = bsz * seq

    z = _filter_positions(seq)
    abs_delta = _decay_rates()
    g1, g1t, t2, t2t = _fft_tables(seq)
    n2 = FFT_N2
    h1 = seq // n2
    n1 = 2 * h1
    nsig = HYENA_ORDER * N_DIRS
    row = lambda v: v[None, :]
    bf = lambda w: w.astype(BF16)

    ffn1 = (bf(ffn1_w_gate), bf(ffn1_w_up), bf(ffn1_w_down))
    ffn2 = (bf(ffn2_w_gate), bf(ffn2_w_up), bf(ffn2_w_down))
    w_in, p_a, p_b, w_out, sgu_w_s = bf(w_in), bf(p_a), bf(p_b), bf(w_out), bf(sgu_w_s)

    xs = x.reshape(n_tok, d)
    for i in range(depth):
        xs = _ffn(xs, i, row(ffn1_pre_g[i]), *ffn1, row(ffn1_post_g[i]))
        pre_g = row(mix_pre_g[i])

        u = _proj_conv(xs, i, pre_g, w_in, hy_conv_w[i], row(hy_conv_b[i]), seq)
        u = u.reshape(HYENA_ORDER + 1, bsz, h1, n2, D_HYENA)
        w1 = jnp.pad(filt_w1[i], ((0, LANES - FILTER_EMB), (0, 0)))
        taps = _filter_taps(z, w1, filt_b[i], filt_freq[i], filt_w_inner[i], filt_w_out[i], abs_delta)
        taps_a = _stage1(g1, taps.reshape(1, nsig, h1, n2, D_HYENA), 0)
        k_spec = _filter_spectrum(t2, taps_a.reshape(nsig, h1, 2 * n2, D_HYENA), seq)
        y, y_part = u, 0
        a1 = _stage1(g1, y, y_part)
        for o in range(HYENA_ORDER):
            last = o == HYENA_ORDER - 1
            b2 = _spectral(t2, t2t, a1.reshape(bsz, h1, 2 * n2, D_HYENA), k_spec, o)
            outs = _inverse_gate(g1t, b2.reshape(bsz, n1, n2, D_HYENA), u, o + 1, y, y_part,
                                 row(hy_skip[i, o]), g1=None if last else g1)
            y, y_part = outs[0], 0
            a1 = None if last else outs[1]
        a = y.reshape(n_tok, D_HYENA)

        b_s = jnp.broadcast_to(sgu_b_s[i][:, :, None], (SGU_GROUPS, SGU_CHUNK, D_SGU // SGU_GROUPS))
        mb = _sgu(xs, i, pre_g, w_in, row(sgu_ln_g[i]), row(sgu_ln_b[i]), sgu_w_s, b_s, p_b)

        xs = _merge(xs, i, pre_g, w_in, a, p_a, mb, w_out, row(mix_post_g[i]))
        xs = _ffn(xs, i, row(ffn2_pre_g[i]), *ffn2, row(ffn2_post_g[i]))
    return xs.reshape(bsz, seq, d)
```

```python
import functools
import math

import jax
import jax.numpy as jnp
from jax import lax
from jax.experimental import pallas as pl
from jax.experimental.pallas import tpu as pltpu

F32 = jnp.float32
BF16 = jnp.bfloat16

D_MODEL = 2048
D_FF = 5632
D_HYENA = D_MODEL // 2
HYENA_ORDER = 2
SHORT_CONV = 3
FILTER_BANDS = 16
FILTER_EMB = 1 + 2 * FILTER_BANDS
FILTER_HIDDEN = 64
FILTER_INNER = 2
DECAY_FAST_PCT = 0.3
DECAY_SLOW_PCT = 1.5
DECAY_TARGET = 1e-2
N_DIRS = 2
D_SGU = D_MODEL // 2
SGU_CHUNK = 128
SGU_GROUPS = 8
NORM_EPS = 1e-6
LN_EPS = 1e-5

LANES = 128
SUBLANES = 8
VMEM_LIMIT_BYTES = 56 * 1024 * 1024

FFN_TM = 512
FFN_TF = 512
PROJ_TM = 512
MIX_TM = 512
FILT_TM = 512
FFT_N2 = 128
FFT_ROWS = 16
FFT_GROUPS = 2
SPECTRAL_K1 = 2
ROW_CHUNK = 16
ROW_UNROLL = True
W_IN_BLOCK = 1024


def _params(*sem):
    return pltpu.CompilerParams(dimension_semantics=sem, vmem_limit_bytes=VMEM_LIMIT_BYTES)


def _resident(shape, index_map):
    return pl.BlockSpec(shape, index_map, pipeline_mode=pl.Buffered(1))


def _rms(x):
    return x * lax.rsqrt(jnp.mean(x * x, axis=-1, keepdims=True) + NORM_EPS)


def _dot(a, b):
    return jnp.dot(a, b, preferred_element_type=F32)


def _row_chunks(n_rows, body, unroll=ROW_UNROLL):
    def step(r, carry):
        body(pl.ds(pl.multiple_of(r * ROW_CHUNK, ROW_CHUNK), ROW_CHUNK))
        return carry
    lax.fori_loop(0, n_rows // ROW_CHUNK, step, 0, unroll=unroll)


def _rms_scales(src_ref, scale_ref):
    def body(rows):
        a = src_ref[rows, :]
        scale_ref[rows, :] = lax.rsqrt(jnp.mean(a * a, axis=-1, keepdims=True) + NORM_EPS)
    _row_chunks(src_ref.shape[0], body)


def _ffn_kernel(x_ref, xnext_ref, pre_g_ref, wg_ref, wu_ref, wd_ref, post_g_ref, o_ref,
                xn_ref, acc_ref, scale_ref, *, look_rows):
    i = pl.program_id(0)
    j = pl.program_id(1)
    tm = x_ref.shape[0]
    cur = lax.rem(i, 2)
    gain = pre_g_ref[...]

    @pl.when(jnp.logical_and(i == 0, j == 0))
    def _():
        _rms_scales(x_ref, scale_ref)

        def body(rows):
            xn_ref[0, rows, :] = (x_ref[rows, :] * scale_ref[rows, :] * gain).astype(BF16)
        _row_chunks(tm, body)

    def step(first):
        xn = xn_ref[cur]
        g = _dot(xn, wg_ref[...])
        u = _dot(xn, wu_ref[...])
        h = (g * jax.nn.sigmoid(g) * u).astype(BF16)
        d = _dot(h, wd_ref[...])
        acc_ref[...] = d if first else acc_ref[...] + d
        start = pl.multiple_of(jnp.minimum(j * look_rows, tm - look_rows), ROW_CHUNK)
        rows = pl.ds(start, look_rows)
        xn_ref[1 - cur, rows, :] = (_rms(xnext_ref[rows, :]) * gain).astype(BF16)

    pl.when(j == 0)(functools.partial(step, True))
    pl.when(j > 0)(functools.partial(step, False))

    @pl.when(j == pl.num_programs(1) - 1)
    def _():
        _rms_scales(acc_ref, scale_ref)
        half_gain = 0.5 * post_g_ref[...]

        def body(rows):
            o_ref[rows, :] = x_ref[rows, :] + acc_ref[rows, :] * scale_ref[rows, :] * half_gain
        _row_chunks(tm, body)


def _ffn(x, layer, pre_g, wg, wu, wd, post_g):
    n, d = x.shape
    f = wg.shape[2]
    tm, tf = FFN_TM, FFN_TF
    steps = f // tf
    look_rows = -(-tm // (steps * ROW_CHUNK)) * ROW_CHUNK
    last_tile = n // tm - 1
    return pl.pallas_call(
        functools.partial(_ffn_kernel, look_rows=look_rows),
        grid=(n // tm, steps),
        in_specs=[
            pl.BlockSpec((tm, d), lambda i, j: (i, 0)),
            pl.BlockSpec((tm, d), lambda i, j: (jnp.minimum(i + 1, last_tile), 0)),
            pl.BlockSpec((1, d), lambda i, j: (0, 0)),
            pl.BlockSpec((None, d, tf), lambda i, j: (layer, 0, j)),
            pl.BlockSpec((None, d, tf), lambda i, j: (layer, 0, j)),
            pl.BlockSpec((None, tf, d), lambda i, j: (layer, j, 0)),
            pl.BlockSpec((1, d), lambda i, j: (0, 0)),
        ],
        out_specs=pl.BlockSpec((tm, d), lambda i, j: (i, 0)),
        out_shape=jax.ShapeDtypeStruct((n, d), F32),
        scratch_shapes=[pltpu.VMEM((2, tm, d), BF16), pltpu.VMEM((tm, d), F32), pltpu.VMEM((tm, 1), F32)],
        compiler_params=_params("arbitrary", "arbitrary"),
        name="ffn",
    )(x, x, pre_g, wg, wu, wd, post_g)


def _proj_conv_kernel(xp_ref, x_ref, xn_ref, g_ref, w_ref, cw_ref, cb_ref, o_ref, *, tiles_per_seq):
    tm = x_ref.shape[0]
    pos = lax.rem(pl.program_id(0), tiles_per_seq)
    keep_prev = (pos != 0).astype(F32)
    keep_next = (pos != tiles_per_seq - 1).astype(F32)
    xa = jnp.concatenate([xp_ref[...] * keep_prev, x_ref[...], xn_ref[...] * keep_next], axis=0)
    xn = (_rms(xa) * g_ref[...]).astype(BF16)
    rows = xn.shape[0]
    mid = slice(SUBLANES, SUBLANES + tm)
    for part in range(o_ref.shape[0]):
        cols = slice(part * D_HYENA, (part + 1) * D_HYENA)
        p = _dot(xn, w_ref[:, cols])
        prev = pltpu.roll(p, 1, 0)[mid]
        nxt = pltpu.roll(p, rows - 1, 0)[mid]
        cw = cw_ref[:, cols]
        y = cb_ref[:, cols] + prev * cw[0:1] + p[mid] * cw[1:2] + nxt * cw[2:3]
        o_ref[part] = y.astype(o_ref.dtype)


def _proj_conv(x, layer, g, w_in, conv_w, conv_b, seq):
    n, d = x.shape
    parts = HYENA_ORDER + 1
    c = parts * D_HYENA
    tm = PROJ_TM
    halo_per_tile = tm // SUBLANES
    last_halo = n // SUBLANES - 1
    const2 = lambda i: (0, 0)
    return pl.pallas_call(
        functools.partial(_proj_conv_kernel, tiles_per_seq=seq // tm),
        grid=(n // tm,),
        in_specs=[
            pl.BlockSpec((SUBLANES, d), lambda i: (jnp.maximum(i * halo_per_tile - 1, 0), 0)),
            pl.BlockSpec((tm, d), lambda i: (i, 0)),
            pl.BlockSpec((SUBLANES, d), lambda i: (jnp.minimum((i + 1) * halo_per_tile, last_halo), 0)),
            pl.BlockSpec((1, d), const2),
            _resident((None, d, c), lambda i: (layer, 0, 0)),
            pl.BlockSpec((SHORT_CONV, c), const2),
            pl.BlockSpec((1, c), const2),
        ],
        out_specs=pl.BlockSpec((parts, tm, D_HYENA), lambda i: (0, i, 0)),
        out_shape=jax.ShapeDtypeStruct((parts, n, D_HYENA), BF16),
        compiler_params=_params("parallel"),
        name="hyena_proj_conv",
    )(x, x, x, g, w_in, conv_w, conv_b)


def _filter_kernel(z_ref, w1_ref, b_ref, freq_ref, wi_ref, wo_ref, delta_ref, o_ref):
    hp = functools.partial(jnp.dot, preferred_element_type=F32, precision=lax.Precision.HIGHEST)
    z = z_ref[...]
    b = b_ref[...]
    freq = freq_ref[...]
    h = jnp.sin(freq[0:1] * (hp(z, w1_ref[...]) + b[0:1]))
    for j in range(FILTER_INNER):
        h = jnp.sin(freq[j + 1:j + 2] * (hp(h, wi_ref[j]) + b[j + 1:j + 2]))
    h = _dot(h.astype(BF16), wo_ref[...].astype(BF16))
    t = z[:, 0:1]
    decay = jnp.exp(-t * delta_ref[...])
    tm = z.shape[0]
    row = pl.program_id(0) * tm + lax.broadcasted_iota(jnp.int32, (tm, D_HYENA), 0)
    for blk in range(HYENA_ORDER * N_DIRS):
        hb = h[:, blk * D_HYENA:(blk + 1) * D_HYENA] * decay
        if blk % N_DIRS == 1:
            hb = jnp.where(row == 0, 0.0, hb)
        o_ref[blk] = hb.astype(o_ref.dtype)


def _filter_taps(z, w1, b, freq, w_inner, w_out, abs_delta):
    seq = z.shape[0]
    nsig = HYENA_ORDER * N_DIRS
    tm = FILT_TM
    full = lambda a: pl.BlockSpec(a.shape, lambda i: (0,) * a.ndim)
    return pl.pallas_call(
        _filter_kernel,
        grid=(seq // tm,),
        in_specs=[pl.BlockSpec((tm, z.shape[1]), lambda i: (i, 0)),
                  full(w1), full(b), full(freq), full(w_inner), full(w_out), full(abs_delta)],
        out_specs=pl.BlockSpec((nsig, tm, D_HYENA), lambda i: (0, i, 0)),
        out_shape=jax.ShapeDtypeStruct((nsig, seq, D_HYENA), BF16),
        compiler_params=_params("parallel"),
        name="filter_taps",
    )(z, w1, b, freq, w_inner, w_out, abs_delta)


def _slab_groups(rows_per_block):
    return [slice(g * FFT_ROWS, (g + 1) * FFT_ROWS) for g in range(rows_per_block // FFT_ROWS)]


def _stage1_kernel(g_ref, x_ref, o_ref):
    h1, rb, c = x_ref.shape[1:]
    n1 = o_ref.shape[1]
    for rows in _slab_groups(rb):
        x = x_ref[0, :, rows, :].reshape(h1 * FFT_ROWS, c)
        o_ref[0, :, rows, :] = _dot(g_ref[...], x).astype(o_ref.dtype).reshape(n1, FFT_ROWS, c)


def _stage1(g1, x, part):
    _, nsig, h1, n2, c = x.shape
    r = FFT_ROWS * FFT_GROUPS
    n1 = g1.shape[0] // FFT_ROWS
    return pl.pallas_call(
        _stage1_kernel,
        grid=(nsig, n2 // r),
        in_specs=[
            pl.BlockSpec(g1.shape, lambda s, j: (0, 0)),
            pl.BlockSpec((None, 1, h1, r, c), lambda s, j: (part, s, 0, j, 0)),
        ],
        out_specs=pl.BlockSpec((1, n1, r, c), lambda s, j: (s, 0, j, 0)),
        out_shape=jax.ShapeDtypeStruct((nsig, n1, n2, c), BF16),
        compiler_params=_params("parallel", "parallel"),
        name="fft_stage1",
    )(g1, x)


def _filter_spectrum_kernel(t_ref, a_ref, k_ref, *, scale):
    t = t_ref[0]
    half = t.shape[0] // 2
    for o in range(HYENA_ORDER):
        xf = _dot(t, a_ref[N_DIRS * o, 0])
        xb = _dot(t, a_ref[N_DIRS * o + 1, 0])
        k_ref[o, 0, :half, :] = (xf[:half] + xb[:half]) * scale
        k_ref[o, 0, half:, :] = (xf[half:] - xb[half:]) * scale


def _filter_spectrum(t2, a, seq):
    nsig, h1, rows, c = a.shape
    return pl.pallas_call(
        functools.partial(_filter_spectrum_kernel, scale=1.0 / seq),
        grid=(h1,),
        in_specs=[
            pl.BlockSpec((1, rows, rows), lambda k: (k, 0, 0)),
            pl.BlockSpec((nsig, 1, rows, c), lambda k: (0, k, 0, 0)),
        ],
        out_specs=pl.BlockSpec((HYENA_ORDER, 1, rows, c), lambda k: (0, k, 0, 0)),
        out_shape=jax.ShapeDtypeStruct((HYENA_ORDER, h1, rows, c), F32),
        compiler_params=_params("parallel"),
        name="filter_spectrum",
    )(t2, a)


def _spectral_kernel(t_ref, tt_ref, a_ref, k_ref, o_ref):
    for q in range(t_ref.shape[0]):
        t = t_ref[q]
        tt = tt_ref[q]
        k = k_ref[0, q]
        half = k.shape[0] // 2
        kr, ki = k[:half], k[half:]
        for b in range(a_ref.shape[0]):
            x = _dot(t, a_ref[b, q])
            xr, xi = x[:half], x[half:]
            z = jnp.concatenate([xr * kr - xi * ki, xr * ki + xi * kr], axis=0).astype(BF16)
            o_ref[b, q] = _dot(tt, z).astype(o_ref.dtype)


def _spectral(t2, t2t, a, k, order):
    bsz, h1, rows, c = a.shape
    kb = SPECTRAL_K1
    return pl.pallas_call(
        _spectral_kernel,
        grid=(h1 // kb,),
        in_specs=[
            pl.BlockSpec((kb, rows, rows), lambda q: (q, 0, 0)),
            pl.BlockSpec((kb, rows, rows), lambda q: (q, 0, 0)),
            pl.BlockSpec((bsz, kb, rows, c), lambda q: (0, q, 0, 0)),
            pl.BlockSpec((1, kb, rows, c), lambda q: (order, q, 0, 0)),
        ],
        out_specs=pl.BlockSpec((bsz, kb, rows, c), lambda q: (0, q, 0, 0)),
        out_shape=jax.ShapeDtypeStruct((bsz, h1, rows, c), BF16),
        compiler_params=_params("parallel"),
        name="hyena_spectral",
    )(t2, t2t, a, k)


def _inverse_gate_kernel(gt_ref, b_ref, gate_ref, y_ref, skip_ref, *rest, emit_stage1):
    if emit_stage1:
        g_ref, o_ref, a_ref = rest
    else:
        (o_ref,), a_ref = rest, None
    n1, rb, c = b_ref.shape[1:]
    h1 = o_ref.shape[2]
    for rows in _slab_groups(rb):
        conv = _dot(gt_ref[...], b_ref[0, :, rows, :].reshape(n1 * FFT_ROWS, c))
        y = y_ref[0, :, rows, :].reshape(conv.shape).astype(F32)
        gate = gate_ref[0, :, rows, :].reshape(conv.shape).astype(F32)
        out = (gate * (conv + y * skip_ref[...])).astype(BF16)
        o_ref[0, 0, :, rows, :] = out.reshape(h1, FFT_ROWS, c)
        if a_ref is not None:
            a_ref[0, :, rows, :] = _dot(g_ref[...], out).astype(BF16).reshape(n1, FFT_ROWS, c)


def _inverse_gate(g1t, b, gate, gate_part, y, y_part, skip, g1=None):
    bsz, n1, n2, c = b.shape
    r = FFT_ROWS * FFT_GROUPS
    h1 = g1t.shape[0] // FFT_ROWS
    const2 = lambda s, j: (0, 0)
    in_specs = [
        pl.BlockSpec(g1t.shape, const2),
        pl.BlockSpec((1, n1, r, c), lambda s, j: (s, 0, j, 0)),
        pl.BlockSpec((None, 1, h1, r, c), lambda s, j: (gate_part, s, 0, j, 0)),
        pl.BlockSpec((None, 1, h1, r, c), lambda s, j: (y_part, s, 0, j, 0)),
        pl.BlockSpec((1, c), const2),
    ]
    out_specs = [pl.BlockSpec((1, 1, h1, r, c), lambda s, j: (0, s, 0, j, 0))]
    out_shape = [jax.ShapeDtypeStruct((1, bsz, h1, n2, c), BF16)]
    args = [g1t, b, gate, y, skip]
    if g1 is not None:
        in_specs.append(pl.BlockSpec(g1.shape, const2))
        out_specs.append(pl.BlockSpec((1, n1, r, c), lambda s, j: (s, 0, j, 0)))
        out_shape.append(jax.ShapeDtypeStruct((bsz, n1, n2, c), BF16))
        args.append(g1)
    return pl.pallas_call(
        functools.partial(_inverse_gate_kernel, emit_stage1=g1 is not None),
        grid=(bsz, n2 // r),
        in_specs=in_specs,
        out_specs=out_specs,
        out_shape=out_shape,
        compiler_params=_params("parallel", "parallel"),
        name="hyena_inverse_gate",
    )(*args)


def _sgu_kernel(x_ref, g_ref, wu_ref, wv_ref, wg0_ref, wg1_ref, lng_ref, lnb_ref, ws_ref, bs_ref,
                pb_ref, o_ref, s_ref):
    xn = (_rms(x_ref[...]) * g_ref[...]).astype(BF16)
    u = jax.nn.gelu(_dot(xn, wu_ref[...]), approximate=True)
    v = jax.nn.gelu(_dot(xn, wv_ref[...]), approximate=True)
    mu = jnp.mean(v, axis=-1, keepdims=True)
    vc = v - mu
    var = jnp.mean(vc * vc, axis=-1, keepdims=True)
    v = (vc * lax.rsqrt(var + LN_EPS) * lng_ref[...] + lnb_ref[...]).astype(BF16)
    dg = D_SGU // SGU_GROUPS
    for c in range(x_ref.shape[0] // SGU_CHUNK):
        rows = slice(c * SGU_CHUNK, (c + 1) * SGU_CHUNK)
        for grp in range(SGU_GROUPS):
            cols = slice(grp * dg, (grp + 1) * dg)
            s_ref[rows, cols] = _dot(ws_ref[grp], v[rows, cols]) + bs_ref[grp]
    gated = (u * s_ref[...]).astype(BF16)
    branch = _dot(gated, pb_ref[...])
    for h, wg_ref in enumerate((wg0_ref, wg1_ref)):
        cols = slice(h * W_IN_BLOCK, (h + 1) * W_IN_BLOCK)
        gate = jax.nn.sigmoid(_dot(xn, wg_ref[...]))
        o_ref[:, cols] = (gate * branch[:, cols]).astype(o_ref.dtype)


def _sgu(x, layer, g, w_in, ln_g, ln_b, w_s, b_s, p_b):
    n, d = x.shape
    tm = MIX_TM
    const2 = lambda i: (0, 0)
    first = (HYENA_ORDER + 1) * D_HYENA // W_IN_BLOCK
    gate_b = first + 2 * D_SGU // W_IN_BLOCK + d // W_IN_BLOCK
    w_blk = lambda blk: _resident((None, d, W_IN_BLOCK), lambda i: (layer, 0, blk))
    return pl.pallas_call(
        _sgu_kernel,
        grid=(n // tm,),
        in_specs=[
            pl.BlockSpec((tm, d), lambda i: (i, 0)),
            pl.BlockSpec((1, d), const2),
            w_blk(first), w_blk(first + 1), w_blk(gate_b), w_blk(gate_b + 1),
            pl.BlockSpec(ln_g.shape, const2),
            pl.BlockSpec(ln_b.shape, const2),
            _resident((None,) + w_s.shape[1:], lambda i: (layer, 0, 0, 0)),
            _resident(b_s.shape, lambda i: (0, 0, 0)),
            _resident((None,) + p_b.shape[1:], lambda i: (layer, 0, 0)),
        ],
        out_specs=pl.BlockSpec((tm, d), lambda i: (i, 0)),
        out_shape=jax.ShapeDtypeStruct((n, d), BF16),
        scratch_shapes=[pltpu.VMEM((tm, D_SGU), F32)],
        compiler_params=_params("parallel"),
        name="sgu_branch",
    )(x, g, w_in, w_in, w_in, w_in, ln_g, ln_b, w_s, b_s, p_b)


def _merge_kernel(x_ref, g_ref, wg0_ref, wg1_ref, a_ref, pa_ref, mb_ref, wout_ref, post_g_ref, o_ref):
    x = x_ref[...]
    xn = (_rms(x) * g_ref[...]).astype(BF16)
    pa = _dot(a_ref[...], pa_ref[...])
    halves = []
    for h, wg_ref in enumerate((wg0_ref, wg1_ref)):
        cols = slice(h * W_IN_BLOCK, (h + 1) * W_IN_BLOCK)
        gate = jax.nn.sigmoid(_dot(xn, wg_ref[...]))
        halves.append((gate * pa[:, cols] + mb_ref[:, cols].astype(F32)).astype(BF16))
    y = _dot(jnp.concatenate(halves, axis=1), wout_ref[...])
    o_ref[...] = x + _rms(y) * post_g_ref[...]


def _merge(x, layer, g, w_in, a, p_a, mb, w_out, post_g):
    n, d = x.shape
    tm = MIX_TM
    const2 = lambda i: (0, 0)
    gate_a = ((HYENA_ORDER + 1) * D_HYENA + 2 * D_SGU) // W_IN_BLOCK
    w_blk = lambda blk: _resident((None, d, W_IN_BLOCK), lambda i: (layer, 0, blk))
    return pl.pallas_call(
        _merge_kernel,
        grid=(n // tm,),
        in_specs=[
            pl.BlockSpec((tm, d), lambda i: (i, 0)),
            pl.BlockSpec((1, d), const2),
            w_blk(gate_a), w_blk(gate_a + 1),
            pl.BlockSpec((tm, a.shape[1]), lambda i: (i, 0)),
            _resident((None,) + p_a.shape[1:], lambda i: (layer, 0, 0)),
            pl.BlockSpec((tm, d), lambda i: (i, 0)),
            _resident((None,) + w_out.shape[1:], lambda i: (layer, 0, 0)),
            pl.BlockSpec((1, d), const2),
        ],
        out_specs=pl.BlockSpec((tm, d), lambda i: (i, 0)),
        out_shape=jax.ShapeDtypeStruct((n, d), F32),
        compiler_params=_params("parallel"),
        name="merge",
    )(x, g, w_in, w_in, a, p_a, mb, w_out, post_g)


def _filter_positions(seq):
    t = jnp.linspace(0.0, 1.0, seq, dtype=F32)[:, None]
    w = (2.0 * math.pi / seq) * jnp.arange(seq, dtype=F32)[:, None]
    f = jnp.linspace(1e-4, FILTER_BANDS - 1, FILTER_BANDS, dtype=F32)[None, :]
    z = jnp.concatenate([t, jnp.cos(f * w), -jnp.sin(f * w)], axis=-1)
    return jnp.pad(z, ((0, 0), (0, LANES - FILTER_EMB)))


def _decay_rates():
    max_decay = math.log(DECAY_TARGET) / DECAY_FAST_PCT
    min_decay = math.log(DECAY_TARGET) / DECAY_SLOW_PCT
    return jnp.abs(jnp.linspace(min_decay, max_decay, D_HYENA, dtype=F32))[None, :]


def _fft_tables(seq):
    n = 2 * seq
    n2 = FFT_N2
    n1 = n // n2
    h1 = n1 // 2
    k1 = jnp.arange(h1, dtype=jnp.int32)
    m1 = ((2 * k1[:, None] + 1) * k1[None, :]) % (2 * n1)
    a1 = m1.astype(F32) * (math.pi / n1)
    f1 = jnp.stack([jnp.cos(a1), -jnp.sin(a1)], axis=1).reshape(n1, h1)
    q = jnp.arange(n2, dtype=jnp.int32)
    kk = 2 * (k1[:, None, None] + n1 * q[None, :, None]) + 1
    a2 = ((kk * q[None, None, :]) % (2 * n)).astype(F32) * (math.pi / n)
    c, s = jnp.cos(a2), jnp.sin(a2)
    t2 = jnp.concatenate([jnp.concatenate([c, s], axis=2), jnp.concatenate([-s, c], axis=2)], axis=1)
    eye = jnp.eye(FFT_ROWS, dtype=F32)
    g1, g1t = jnp.kron(f1, eye), jnp.kron(f1.T, eye)
    return (g1.astype(BF16), g1t.astype(BF16), t2.astype(BF16), jnp.swapaxes(t2, 1, 2).astype(BF16))


def kernel(x, ffn1_pre_g, ffn1_w_gate, ffn1_w_up, ffn1_w_down, ffn1_post_g, mix_pre_g, w_in, hy_conv_w, hy_conv_b, filt_w1, filt_b, filt_freq, filt_w_inner, filt_w_out, hy_skip, sgu_ln_g, sgu_ln_b, sgu_w_s, sgu_b_s, p_a, p_b, w_out, mix_post_g, ffn2_pre_g, ffn2_w_gate, ffn2_w_up, ffn2_w_down, ffn2_post_g):
    bsz, seq, d = x.shape
    depth = w_in.shape[0]
    n_tok = bsz * seq

    z = _filter_positions(seq)
    abs_delta = _decay_rates()
    g1, g1t, t2, t2t = _fft_tables(seq)
    n2 = FFT_N2
    h1 = seq // n2
    n1 = 2 * h1
    nsig = HYENA_ORDER * N_DIRS
    row = lambda v: v[None, :]
    bf = lambda w: w.astype(BF16)

    ffn1 = (bf(ffn1_w_gate), bf(ffn1_w_up), bf(ffn1_w_down))
    ffn2 = (bf(ffn2_w_gate), bf(ffn2_w_up), bf(ffn2_w_down))
    w_in, p_a, p_b, w_out, sgu_w_s = bf(w_in), bf(p_a), bf(p_b), bf(w_out), bf(sgu_w_s)

    xs = x.reshape(n_tok, d)
    for i in range(depth):
        xs = _ffn(xs, i, row(ffn1_pre_g[i]), *ffn1, row(ffn1_post_g[i]))
        pre_g = row(mix_pre_g[i])

        u = _proj_conv(xs, i, pre_g, w_in, hy_conv_w[i], row(hy_conv_b[i]), seq)
        u = u.reshape(HYENA_ORDER + 1, bsz, h1, n2, D_HYENA)
        w1 = jnp.pad(filt_w1[i], ((0, LANES - FILTER_EMB), (0, 0)))
        taps = _filter_taps(z, w1, filt_b[i], filt_freq[i], filt_w_inner[i], filt_w_out[i], abs_delta)
        taps_a = _stage1(g1, taps.reshape(1, nsig, h1, n2, D_HYENA), 0)
        k_spec = _filter_spectrum(t2, taps_a.reshape(nsig, h1, 2 * n2, D_HYENA), seq)
        y, y_part = u, 0
        a1 = _stage1(g1, y, y_part)
        for o in range(HYENA_ORDER):
            last = o == HYENA_ORDER - 1
            b2 = _spectral(t2, t2t, a1.reshape(bsz, h1, 2 * n2, D_HYENA), k_spec, o)
            outs = _inverse_gate(g1t, b2.reshape(bsz, n1, n2, D_HYENA), u, o + 1, y, y_part,
                                 row(hy_skip[i, o]), g1=None if last else g1)
            y, y_part = outs[0], 0
            a1 = None if last else outs[1]
        a = y.reshape(n_tok, D_HYENA)

        b_s = jnp.broadcast_to(sgu_b_s[i][:, :, None], (SGU_GROUPS, SGU_CHUNK, D_SGU // SGU_GROUPS))
        mb = _sgu(xs, i, pre_g, w_in, row(sgu_ln_g[i]), row(sgu_ln_b[i]), sgu_w_s, b_s, p_b)

        xs = _merge(xs, i, pre_g, w_in, a, p_a, mb, w_out, row(mix_post_g[i]))
        xs = _ffn(xs, i, row(ffn2_pre_g[i]), *ffn2, row(ffn2_post_g[i]))
    return xs.reshape(bsz, seq, d)
```

```python
import functools
import math

import jax
import jax.numpy as jnp
from jax import lax
from jax.experimental import pallas as pl
from jax.experimental.pallas import tpu as pltpu

F32 = jnp.float32
BF16 = jnp.bfloat16

D_MODEL = 2048
D_FF = 5632
D_HYENA = D_MODEL // 2
HYENA_ORDER = 2
SHORT_CONV = 3
FILTER_BANDS = 16
FILTER_EMB = 1 + 2 * FILTER_BANDS
FILTER_HIDDEN = 64
FILTER_INNER = 2
DECAY_FAST_PCT = 0.3
DECAY_SLOW_PCT = 1.5
DECAY_TARGET = 1e-2
N_DIRS = 2
D_SGU = D_MODEL // 2
SGU_CHUNK = 128
SGU_GROUPS = 8
NORM_EPS = 1e-6
LN_EPS = 1e-5

LANES = 128
SUBLANES = 8
VMEM_LIMIT_BYTES = 56 * 1024 * 1024

FFN_TM = 512
FFN_TF = 512
PROJ_TM = 512
MIX_TM = 512
FILT_TM = 512
FFT_N2 = 128
FFT_ROWS = 16
FFT_GROUPS = 2
SPECTRAL_K1 = 2
ROW_CHUNK = 16
ROW_UNROLL = True
W_IN_BLOCK = 1024


def _params(*sem):
    return pltpu.CompilerParams(dimension_semantics=sem, vmem_limit_bytes=VMEM_LIMIT_BYTES)


def _resident(shape, index_map):
    return pl.BlockSpec(shape, index_map, pipeline_mode=pl.Buffered(1))


def _rms(x):
    return x * lax.rsqrt(jnp.mean(x * x, axis=-1, keepdims=True) + NORM_EPS)


def _dot(a, b):
    return jnp.dot(a, b, preferred_element_type=F32)


def _row_chunks(n_rows, body, unroll=ROW_UNROLL):
    def step(r, carry):
        body(pl.ds(pl.multiple_of(r * ROW_CHUNK, ROW_CHUNK), ROW_CHUNK))
        return carry
    lax.fori_loop(0, n_rows // ROW_CHUNK, step, 0, unroll=unroll)


def _rms_scales(src_ref, scale_ref):
    def body(rows):
        a = src_ref[rows, :]
        scale_ref[rows, :] = lax.rsqrt(jnp.mean(a * a, axis=-1, keepdims=True) + NORM_EPS)
    _row_chunks(src_ref.shape[0], body)


def _ffn_kernel(x_ref, pre_g_ref, wg_ref, wu_ref, wd_ref, post_g_ref, o_ref, xn_ref, acc_ref, scale_ref):
    j = pl.program_id(1)
    tm = x_ref.shape[0]

    @pl.when(j == 0)
    def _():
        _rms_scales(x_ref, scale_ref)
        gain = pre_g_ref[...]

        def body(rows):
            xn_ref[rows, :] = (x_ref[rows, :] * scale_ref[rows, :] * gain).astype(BF16)
        _row_chunks(tm, body)

    def step(first):
        xn = xn_ref[...]
        g = _dot(xn, wg_ref[...])
        u = _dot(xn, wu_ref[...])
        h = (g * jax.nn.sigmoid(g) * u).astype(BF16)
        d = _dot(h, wd_ref[...])
        acc_ref[...] = d if first else acc_ref[...] + d

    pl.when(j == 0)(functools.partial(step, True))
    pl.when(j > 0)(functools.partial(step, False))

    @pl.when(j == pl.num_programs(1) - 1)
    def _():
        _rms_scales(acc_ref, scale_ref)
        half_gain = 0.5 * post_g_ref[...]

        def body(rows):
            o_ref[rows, :] = x_ref[rows, :] + acc_ref[rows, :] * scale_ref[rows, :] * half_gain
        _row_chunks(tm, body)


def _ffn(x, layer, pre_g, wg, wu, wd, post_g):
    n, d = x.shape
    f = wg.shape[2]
    tm, tf = FFN_TM, FFN_TF
    return pl.pallas_call(
        _ffn_kernel,
        grid=(n // tm, f // tf),
        in_specs=[
            pl.BlockSpec((tm, d), lambda i, j: (i, 0)),
            pl.BlockSpec((1, d), lambda i, j: (0, 0)),
            pl.BlockSpec((None, d, tf), lambda i, j: (layer, 0, j)),
            pl.BlockSpec((None, d, tf), lambda i, j: (layer, 0, j)),
            pl.BlockSpec((None, tf, d), lambda i, j: (layer, j, 0)),
            pl.BlockSpec((1, d), lambda i, j: (0, 0)),
        ],
        out_specs=pl.BlockSpec((tm, d), lambda i, j: (i, 0)),
        out_shape=jax.ShapeDtypeStruct((n, d), F32),
        scratch_shapes=[pltpu.VMEM((tm, d), BF16), pltpu.VMEM((tm, d), F32), pltpu.VMEM((tm, 1), F32)],
        compiler_params=_params("parallel", "arbitrary"),
        name="ffn",
    )(x, pre_g, wg, wu, wd, post_g)


def _proj_conv_kernel(xp_ref, x_ref, xn_ref, g_ref, w_ref, cw_ref, cb_ref, o_ref, *, tiles_per_seq):
    tm = x_ref.shape[0]
    pos = lax.rem(pl.program_id(0), tiles_per_seq)
    keep_prev = (pos != 0).astype(F32)
    keep_next = (pos != tiles_per_seq - 1).astype(F32)
    xa = jnp.concatenate([xp_ref[...] * keep_prev, x_ref[...], xn_ref[...] * keep_next], axis=0)
    xn = (_rms(xa) * g_ref[...]).astype(BF16)
    rows = xn.shape[0]
    mid = slice(SUBLANES, SUBLANES + tm)
    for part in range(o_ref.shape[0]):
        cols = slice(part * D_HYENA, (part + 1) * D_HYENA)
        p = _dot(xn, w_ref[:, cols])
        prev = pltpu.roll(p, 1, 0)[mid]
        nxt = pltpu.roll(p, rows - 1, 0)[mid]
        cw = cw_ref[:, cols]
        y = cb_ref[:, cols] + prev * cw[0:1] + p[mid] * cw[1:2] + nxt * cw[2:3]
        o_ref[part] = y.astype(o_ref.dtype)


def _proj_conv(x, layer, g, w_in, conv_w, conv_b, seq):
    n, d = x.shape
    parts = HYENA_ORDER + 1
    c = parts * D_HYENA
    tm = PROJ_TM
    halo_per_tile = tm // SUBLANES
    last_halo = n // SUBLANES - 1
    const2 = lambda i: (0, 0)
    return pl.pallas_call(
        functools.partial(_proj_conv_kernel, tiles_per_seq=seq // tm),
        grid=(n // tm,),
        in_specs=[
            pl.BlockSpec((SUBLANES, d), lambda i: (jnp.maximum(i * halo_per_tile - 1, 0), 0)),
            pl.BlockSpec((tm, d), lambda i: (i, 0)),
            pl.BlockSpec((SUBLANES, d), lambda i: (jnp.minimum((i + 1) * halo_per_tile, last_halo), 0)),
            pl.BlockSpec((1, d), const2),
            _resident((None, d, c), lambda i: (layer, 0, 0)),
            pl.BlockSpec((SHORT_CONV, c), const2),
            pl.BlockSpec((1, c), const2),
        ],
        out_specs=pl.BlockSpec((parts, tm, D_HYENA), lambda i: (0, i, 0)),
        out_shape=jax.ShapeDtypeStruct((parts, n, D_HYENA), BF16),
        compiler_params=_params("parallel"),
        name="hyena_proj_conv",
    )(x, x, x, g, w_in, conv_w, conv_b)


def _filter_kernel(z_ref, w1_ref, b_ref, freq_ref, wi_ref, wo_ref, delta_ref, o_ref):
    hp = functools.partial(jnp.dot, preferred_element_type=F32, precision=lax.Precision.HIGHEST)
    z = z_ref[...]
    b = b_ref[...]
    freq = freq_ref[...]
    h = jnp.sin(freq[0:1] * (hp(z, w1_ref[...]) + b[0:1]))
    for j in range(FILTER_INNER):
        h = jnp.sin(freq[j + 1:j + 2] * (hp(h, wi_ref[j]) + b[j + 1:j + 2]))
    h = _dot(h.astype(BF16), wo_ref[...].astype(BF16))
    t = z[:, 0:1]
    decay = jnp.exp(-t * delta_ref[...])
    tm = z.shape[0]
    row = pl.program_id(0) * tm + lax.broadcasted_iota(jnp.int32, (tm, D_HYENA), 0)
    for blk in range(HYENA_ORDER * N_DIRS):
        hb = h[:, blk * D_HYENA:(blk + 1) * D_HYENA] * decay
        if blk % N_DIRS == 1:
            hb = jnp.where(row == 0, 0.0, hb)
        o_ref[blk] = hb.astype(o_ref.dtype)


def _filter_taps(z, w1, b, freq, w_inner, w_out, abs_delta):
    seq = z.shape[0]
    nsig = HYENA_ORDER * N_DIRS
    tm = FILT_TM
    full = lambda a: pl.BlockSpec(a.shape, lambda i: (0,) * a.ndim)
    return pl.pallas_call(
        _filter_kernel,
        grid=(seq // tm,),
        in_specs=[pl.BlockSpec((tm, z.shape[1]), lambda i: (i, 0)),
                  full(w1), full(b), full(freq), full(w_inner), full(w_out), full(abs_delta)],
        out_specs=pl.BlockSpec((nsig, tm, D_HYENA), lambda i: (0, i, 0)),
        out_shape=jax.ShapeDtypeStruct((nsig, seq, D_HYENA), BF16),
        compiler_params=_params("parallel"),
        name="filter_taps",
    )(z, w1, b, freq, w_inner, w_out, abs_delta)


def _slab_groups(rows_per_block):
    return [slice(g * FFT_ROWS, (g + 1) * FFT_ROWS) for g in range(rows_per_block // FFT_ROWS)]


def _stage1_kernel(g_ref, x_ref, o_ref):
    h1, rb, c = x_ref.shape[1:]
    n1 = o_ref.shape[1]
    for rows in _slab_groups(rb):
        x = x_ref[0, :, rows, :].reshape(h1 * FFT_ROWS, c)
        o_ref[0, :, rows, :] = _dot(g_ref[...], x).astype(o_ref.dtype).reshape(n1, FFT_ROWS, c)


def _stage1(g1, x, part):
    _, nsig, h1, n2, c = x.shape
    r = FFT_ROWS * FFT_GROUPS
    n1 = g1.shape[0] // FFT_ROWS
    return pl.pallas_call(
        _stage1_kernel,
        grid=(nsig, n2 // r),
        in_specs=[
            pl.BlockSpec(g1.shape, lambda s, j: (0, 0)),
            pl.BlockSpec((None, 1, h1, r, c), lambda s, j: (part, s, 0, j, 0)),
        ],
        out_specs=pl.BlockSpec((1, n1, r, c), lambda s, j: (s, 0, j, 0)),
        out_shape=jax.ShapeDtypeStruct((nsig, n1, n2, c), BF16),
        compiler_params=_params("parallel", "parallel"),
        name="fft_stage1",
    )(g1, x)


def _filter_spectrum_kernel(t_ref, a_ref, k_ref, *, scale):
    t = t_ref[0]
    half = t.shape[0] // 2
    for o in range(HYENA_ORDER):
        xf = _dot(t, a_ref[N_DIRS * o, 0])
        xb = _dot(t, a_ref[N_DIRS * o + 1, 0])
        k_ref[o, 0, :half, :] = (xf[:half] + xb[:half]) * scale
        k_ref[o, 0, half:, :] = (xf[half:] - xb[half:]) * scale


def _filter_spectrum(t2, a, seq):
    nsig, h1, rows, c = a.shape
    return pl.pallas_call(
        functools.partial(_filter_spectrum_kernel, scale=1.0 / seq),
        grid=(h1,),
        in_specs=[
            pl.BlockSpec((1, rows, rows), lambda k: (k, 0, 0)),
            pl.BlockSpec((nsig, 1, rows, c), lambda k: (0, k, 0, 0)),
        ],
        out_specs=pl.BlockSpec((HYENA_ORDER, 1, rows, c), lambda k: (0, k, 0, 0)),
        out_shape=jax.ShapeDtypeStruct((HYENA_ORDER, h1, rows, c), F32),
        compiler_params=_params("parallel"),
        name="filter_spectrum",
    )(t2, a)


def _spectral_kernel(t_ref, tt_ref, a_ref, k_ref, o_ref):
    for q in range(t_ref.shape[0]):
        t = t_ref[q]
        tt = tt_ref[q]
        k = k_ref[0, q]
        half = k.shape[0] // 2
        kr, ki = k[:half], k[half:]
        for b in range(a_ref.shape[0]):
            x = _dot(t, a_ref[b, q])
            xr, xi = x[:half], x[half:]
            z = jnp.concatenate([xr * kr - xi * ki, xr * ki + xi * kr], axis=0).astype(BF16)
            o_ref[b, q] = _dot(tt, z).astype(o_ref.dtype)


def _spectral(t2, t2t, a, k, order):
    bsz, h1, rows, c = a.shape
    kb = SPECTRAL_K1
    return pl.pallas_call(
        _spectral_kernel,
        grid=(h1 // kb,),
        in_specs=[
            pl.BlockSpec((kb, rows, rows), lambda q: (q, 0, 0)),
            pl.BlockSpec((kb, rows, rows), lambda q: (q, 0, 0)),
            pl.BlockSpec((bsz, kb, rows, c), lambda q: (0, q, 0, 0)),
            pl.BlockSpec((1, kb, rows, c), lambda q: (order, q, 0, 0)),
        ],
        out_specs=pl.BlockSpec((bsz, kb, rows, c), lambda q: (0, q, 0, 0)),
        out_shape=jax.ShapeDtypeStruct((bsz, h1, rows, c), BF16),
        compiler_params=_params("parallel"),
        name="hyena_spectral",
    )(t2, t2t, a, k)


def _inverse_gate_kernel(gt_ref, b_ref, gate_ref, y_ref, skip_ref, *rest, emit_stage1):
    if emit_stage1:
        g_ref, o_ref, a_ref = rest
    else:
        (o_ref,), a_ref = rest, None
    n1, rb, c = b_ref.shape[1:]
    h1 = o_ref.shape[2]
    for rows in _slab_groups(rb):
        conv = _dot(gt_ref[...], b_ref[0, :, rows, :].reshape(n1 * FFT_ROWS, c))
        y = y_ref[0, :, rows, :].reshape(conv.shape).astype(F32)
        gate = gate_ref[0, :, rows, :].reshape(conv.shape).astype(F32)
        out = (gate * (conv + y * skip_ref[...])).astype(BF16)
        o_ref[0, 0, :, rows, :] = out.reshape(h1, FFT_ROWS, c)
        if a_ref is not None:
            a_ref[0, :, rows, :] = _dot(g_ref[...], out).astype(BF16).reshape(n1, FFT_ROWS, c)


def _inverse_gate(g1t, b, gate, gate_part, y, y_part, skip, g1=None):
    bsz, n1, n2, c = b.shape
    r = FFT_ROWS * FFT_GROUPS
    h1 = g1t.shape[0] // FFT_ROWS
    const2 = lambda s, j: (0, 0)
    in_specs = [
        pl.BlockSpec(g1t.shape, const2),
        pl.BlockSpec((1, n1, r, c), lambda s, j: (s, 0, j, 0)),
        pl.BlockSpec((None, 1, h1, r, c), lambda s, j: (gate_part, s, 0, j, 0)),
        pl.BlockSpec((None, 1, h1, r, c), lambda s, j: (y_part, s, 0, j, 0)),
        pl.BlockSpec((1, c), const2),
    ]
    out_specs = [pl.BlockSpec((1, 1, h1, r, c), lambda s, j: (0, s, 0, j, 0))]
    out_shape = [jax.ShapeDtypeStruct((1, bsz, h1, n2, c), BF16)]
    args = [g1t, b, gate, y, skip]
    if g1 is not None:
        in_specs.append(pl.BlockSpec(g1.shape, const2))
        out_specs.append(pl.BlockSpec((1, n1, r, c), lambda s, j: (s, 0, j, 0)))
        out_shape.append(jax.ShapeDtypeStruct((bsz, n1, n2, c), BF16))
        args.append(g1)
    return pl.pallas_call(
        functools.partial(_inverse_gate_kernel, emit_stage1=g1 is not None),
        grid=(bsz, n2 // r),
        in_specs=in_specs,
        out_specs=out_specs,
        out_shape=out_shape,
        compiler_params=_params("parallel", "parallel"),
        name="hyena_inverse_gate",
    )(*args)


def _sgu_kernel(x_ref, g_ref, wu_ref, wv_ref, wg0_ref, wg1_ref, lng_ref, lnb_ref, ws_ref, bs_ref,
                pb_ref, o_ref, s_ref):
    xn = (_rms(x_ref[...]) * g_ref[...]).astype(BF16)
    u = jax.nn.gelu(_dot(xn, wu_ref[...]), approximate=True)
    v = jax.nn.gelu(_dot(xn, wv_ref[...]), approximate=True)
    mu = jnp.mean(v, axis=-1, keepdims=True)
    vc = v - mu
    var = jnp.mean(vc * vc, axis=-1, keepdims=True)
    v = (vc * lax.rsqrt(var + LN_EPS) * lng_ref[...] + lnb_ref[...]).astype(BF16)
    dg = D_SGU // SGU_GROUPS
    for c in range(x_ref.shape[0] // SGU_CHUNK):
        rows = slice(c * SGU_CHUNK, (c + 1) * SGU_CHUNK)
        for grp in range(SGU_GROUPS):
            cols = slice(grp * dg, (grp + 1) * dg)
            s_ref[rows, cols] = _dot(ws_ref[grp], v[rows, cols]) + bs_ref[grp]
    gated = (u * s_ref[...]).astype(BF16)
    branch = _dot(gated, pb_ref[...])
    for h, wg_ref in enumerate((wg0_ref, wg1_ref)):
        cols = slice(h * W_IN_BLOCK, (h + 1) * W_IN_BLOCK)
        gate = jax.nn.sigmoid(_dot(xn, wg_ref[...]))
        o_ref[:, cols] = (gate * branch[:, cols]).astype(o_ref.dtype)


def _sgu(x, layer, g, w_in, ln_g, ln_b, w_s, b_s, p_b):
    n, d = x.shape
    tm = MIX_TM
    const2 = lambda i: (0, 0)
    first = (HYENA_ORDER + 1) * D_HYENA // W_IN_BLOCK
    gate_b = first + 2 * D_SGU // W_IN_BLOCK + d // W_IN_BLOCK
    w_blk = lambda blk: _resident((None, d, W_IN_BLOCK), lambda i: (layer, 0, blk))
    return pl.pallas_call(
        _sgu_kernel,
        grid=(n // tm,),
        in_specs=[
            pl.BlockSpec((tm, d), lambda i: (i, 0)),
            pl.BlockSpec((1, d), const2),
            w_blk(first), w_blk(first + 1), w_blk(gate_b), w_blk(gate_b + 1),
            pl.BlockSpec(ln_g.shape, const2),
            pl.BlockSpec(ln_b.shape, const2),
            _resident((None,) + w_s.shape[1:], lambda i: (layer, 0, 0, 0)),
            _resident(b_s.shape, lambda i: (0, 0, 0)),
            _resident((None,) + p_b.shape[1:], lambda i: (layer, 0, 0)),
        ],
        out_specs=pl.BlockSpec((tm, d), lambda i: (i, 0)),
        out_shape=jax.ShapeDtypeStruct((n, d), BF16),
        scratch_shapes=[pltpu.VMEM((tm, D_SGU), F32)],
        compiler_params=_params("parallel"),
        name="sgu_branch",
    )(x, g, w_in, w_in, w_in, w_in, ln_g, ln_b, w_s, b_s, p_b)


def _merge_kernel(x_ref, g_ref, wg0_ref, wg1_ref, a_ref, pa_ref, mb_ref, wout_ref, post_g_ref, o_ref):
    x = x_ref[...]
    xn = (_rms(x) * g_ref[...]).astype(BF16)
    pa = _dot(a_ref[...], pa_ref[...])
    halves = []
    for h, wg_ref in enumerate((wg0_ref, wg1_ref)):
        cols = slice(h * W_IN_BLOCK, (h + 1) * W_IN_BLOCK)
        gate = jax.nn.sigmoid(_dot(xn, wg_ref[...]))
        halves.append((gate * pa[:, cols] + mb_ref[:, cols].astype(F32)).astype(BF16))
    y = _dot(jnp.concatenate(halves, axis=1), wout_ref[...])
    o_ref[...] = x + _rms(y) * post_g_ref[...]


def _merge(x, layer, g, w_in, a, p_a, mb, w_out, post_g):
    n, d = x.shape
    tm = MIX_TM
    const2 = lambda i: (0, 0)
    gate_a = ((HYENA_ORDER + 1) * D_HYENA + 2 * D_SGU) // W_IN_BLOCK
    w_blk = lambda blk: _resident((None, d, W_IN_BLOCK), lambda i: (layer, 0, blk))
    return pl.pallas_call(
        _merge_kernel,
        grid=(n // tm,),
        in_specs=[
            pl.BlockSpec((tm, d), lambda i: (i, 0)),
            pl.BlockSpec((1, d), const2),
            w_blk(gate_a), w_blk(gate_a + 1),
            pl.BlockSpec((tm, a.shape[1]), lambda i: (i, 0)),
            _resident((None,) + p_a.shape[1:], lambda i: (layer, 0, 0)),
            pl.BlockSpec((tm, d), lambda i: (i, 0)),
            _resident((None,) + w_out.shape[1:], lambda i: (layer, 0, 0)),
            pl.BlockSpec((1, d), const2),
        ],
        out_specs=pl.BlockSpec((tm, d), lambda i: (i, 0)),
        out_shape=jax.ShapeDtypeStruct((n, d), F32),
        compiler_params=_params("parallel"),
        name="merge",
    )(x, g, w_in, w_in, a, p_a, mb, w_out, post_g)


def _filter_positions(seq):
    t = jnp.linspace(0.0, 1.0, seq, dtype=F32)[:, None]
    w = (2.0 * math.pi / seq) * jnp.arange(seq, dtype=F32)[:, None]
    f = jnp.linspace(1e-4, FILTER_BANDS - 1, FILTER_BANDS, dtype=F32)[None, :]
    z = jnp.concatenate([t, jnp.cos(f * w), -jnp.sin(f * w)], axis=-1)
    return jnp.pad(z, ((0, 0), (0, LANES - FILTER_EMB)))


def _decay_rates():
    max_decay = math.log(DECAY_TARGET) / DECAY_FAST_PCT
    min_decay = math.log(DECAY_TARGET) / DECAY_SLOW_PCT
    return jnp.abs(jnp.linspace(min_decay, max_decay, D_HYENA, dtype=F32))[None, :]


def _fft_tables(seq):
    n = 2 * seq
    n2 = FFT_N2
    n1 = n // n2
    h1 = n1 // 2
    k1 = jnp.arange(h1, dtype=jnp.int32)
    m1 = ((2 * k1[:, None] + 1) * k1[None, :]) % (2 * n1)
    a1 = m1.astype(F32) * (math.pi / n1)
    f1 = jnp.stack([jnp.cos(a1), -jnp.sin(a1)], axis=1).reshape(n1, h1)
    q = jnp.arange(n2, dtype=jnp.int32)
    kk = 2 * (k1[:, None, None] + n1 * q[None, :, None]) + 1
    a2 = ((kk * q[None, None, :]) % (2 * n)).astype(F32) * (math.pi / n)
    c, s = jnp.cos(a2), jnp.sin(a2)
    t2 = jnp.concatenate([jnp.concatenate([c, s], axis=2), jnp.concatenate([-s, c], axis=2)], axis=1)
    eye = jnp.eye(FFT_ROWS, dtype=F32)
    g1, g1t = jnp.kron(f1, eye), jnp.kron(f1.T, eye)
    return (g1.astype(BF16), g1t.astype(BF16), t2.astype(BF16), jnp.swapaxes(t2, 1, 2).astype(BF16))


def kernel(x, ffn1_pre_g, ffn1_w_gate, ffn1_w_up, ffn1_w_down, ffn1_post_g, mix_pre_g, w_in, hy_conv_w, hy_conv_b, filt_w1, filt_b, filt_freq, filt_w_inner, filt_w_out, hy_skip, sgu_ln_g, sgu_ln_b, sgu_w_s, sgu_b_s, p_a, p_b, w_out, mix_post_g, ffn2_pre_g, ffn2_w_gate, ffn2_w_up, ffn2_w_down, ffn2_post_g):
    bsz, seq, d = x.shape
    depth = w_in.shape[0]
    n_tok = bsz * seq

    z = _filter_positions(seq)
    abs_delta = _decay_rates()
    g1, g1t, t2, t2t = _fft_tables(seq)
    n2 = FFT_N2
    h1 = seq // n2
    n1 = 2 * h1
    nsig = HYENA_ORDER * N_DIRS
    row = lambda v: v[None, :]
    bf = lambda w: w.astype(BF16)

    ffn1 = (bf(ffn1_w_gate), bf(ffn1_w_up), bf(ffn1_w_down))
    ffn2 = (bf(ffn2_w_gate), bf(ffn2_w_up), bf(ffn2_w_down))
    w_in, p_a, p_b, w_out, sgu_w_s = bf(w_in), bf(p_a), bf(p_b), bf(w_out), bf(sgu_w_s)

    xs = x.reshape(n_tok, d)
    for i in range(depth):
        xs = _ffn(xs, i, row(ffn1_pre_g[i]), *ffn1, row(ffn1_post_g[i]))
        pre_g = row(mix_pre_g[i])

        u = _proj_conv(xs, i, pre_g, w_in, hy_conv_w[i], row(hy_conv_b[i]), seq)
        u = u.reshape(HYENA_ORDER + 1, bsz, h1, n2, D_HYENA)
        w1 = jnp.pad(filt_w1[i], ((0, LANES - FILTER_EMB), (0, 0)))
        taps = _filter_taps(z, w1, filt_b[i], filt_freq[i], filt_w_inner[i], filt_w_out[i], abs_delta)
        taps_a = _stage1(g1, taps.reshape(1, nsig, h1, n2, D_HYENA), 0)
        k_spec = _filter_spectrum(t2, taps_a.reshape(nsig, h1, 2 * n2, D_HYENA), seq)
        y, y_part = u, 0
        a1 = _stage1(g1, y, y_part)
        for o in range(HYENA_ORDER):
            last = o == HYENA_ORDER - 1
            b2 = _spectral(t2, t2t, a1.reshape(bsz, h1, 2 * n2, D_HYENA), k_spec, o)
            outs = _inverse_gate(g1t, b2.reshape(bsz, n1, n2, D_HYENA), u, o + 1, y, y_part,
                                 row(hy_skip[i, o]), g1=None if last else g1)
            y, y_part = outs[0], 0
            a1 = None if last else outs[1]
        a = y.reshape(n_tok, D_HYENA)

        b_s = jnp.broadcast_to(sgu_b_s[i][:, :, None], (SGU_GROUPS, SGU_CHUNK, D_SGU // SGU_GROUPS))
        mb = _sgu(xs, i, pre_g, w_in, row(sgu_ln_g[i]), row(sgu_ln_b[i]), sgu_w_s, b_s, p_b)

        xs = _merge(xs, i, pre_g, w_in, a, p_a, mb, w_out, row(mix_post_g[i]))
        xs = _ffn(xs, i, row(ffn2_pre_g[i]), *ffn2, row(ffn2_post_g[i]))
    return xs.reshape(bsz, seq, d)
```

```python
import functools
import math

import jax
import jax.numpy as jnp
from jax import lax
from jax.experimental import pallas as pl
from jax.experimental.pallas import tpu as pltpu

F32 = jnp.float32
BF16 = jnp.bfloat16

D_MODEL = 2048
D_FF = 5632
D_HYENA = D_MODEL // 2
HYENA_ORDER = 2
SHORT_CONV = 3
FILTER_BANDS = 16
FILTER_EMB = 1 + 2 * FILTER_BANDS
FILTER_HIDDEN = 64
FILTER_INNER = 2
DECAY_FAST_PCT = 0.3
DECAY_SLOW_PCT = 1.5
DECAY_TARGET = 1e-2
N_DIRS = 2
D_SGU = D_MODEL // 2
SGU_CHUNK = 128
SGU_GROUPS = 8
NORM_EPS = 1e-6
LN_EPS = 1e-5

LANES = 128
SUBLANES = 8
VMEM_LIMIT_BYTES = 56 * 1024 * 1024

FFN_TM = 1024
FFN_TF = 256
PROJ_TM = 512
MIX_TM = 512
FILT_TM = 512
FFT_N2 = 128
FFT_ROWS = 16
FFT_GROUPS = 2
SPECTRAL_K1 = 2
ROW_CHUNK = 16
ROW_UNROLL = True
W_IN_BLOCK = 1024


def _params(*sem):
    return pltpu.CompilerParams(dimension_semantics=sem, vmem_limit_bytes=VMEM_LIMIT_BYTES)


def _resident(shape, index_map):
    return pl.BlockSpec(shape, index_map, pipeline_mode=pl.Buffered(1))


def _rms(x):
    return x * lax.rsqrt(jnp.mean(x * x, axis=-1, keepdims=True) + NORM_EPS)


def _dot(a, b):
    return jnp.dot(a, b, preferred_element_type=F32)


def _row_chunks(n_rows, body, unroll=ROW_UNROLL):
    def step(r, carry):
        body(pl.ds(pl.multiple_of(r * ROW_CHUNK, ROW_CHUNK), ROW_CHUNK))
        return carry
    lax.fori_loop(0, n_rows // ROW_CHUNK, step, 0, unroll=unroll)


def _rms_scales(src_ref, scale_ref):
    def body(rows):
        a = src_ref[rows, :]
        scale_ref[rows, :] = lax.rsqrt(jnp.mean(a * a, axis=-1, keepdims=True) + NORM_EPS)
    _row_chunks(src_ref.shape[0], body)


def _ffn_kernel(x_ref, pre_g_ref, wg_ref, wu_ref, wd_ref, post_g_ref, o_ref, xn_ref, acc_ref, scale_ref):
    j = pl.program_id(1)
    tm = x_ref.shape[0]

    @pl.when(j == 0)
    def _():
        _rms_scales(x_ref, scale_ref)
        gain = pre_g_ref[...]

        def body(rows):
            xn_ref[rows, :] = (x_ref[rows, :] * scale_ref[rows, :] * gain).astype(BF16)
        _row_chunks(tm, body)

    def step(first):
        xn = xn_ref[...]
        g = _dot(xn, wg_ref[...])
        u = _dot(xn, wu_ref[...])
        h = (g * jax.nn.sigmoid(g) * u).astype(BF16)
        d = _dot(h, wd_ref[...])
        acc_ref[...] = d if first else acc_ref[...] + d

    pl.when(j == 0)(functools.partial(step, True))
    pl.when(j > 0)(functools.partial(step, False))

    @pl.when(j == pl.num_programs(1) - 1)
    def _():
        _rms_scales(acc_ref, scale_ref)
        half_gain = 0.5 * post_g_ref[...]

        def body(rows):
            o_ref[rows, :] = x_ref[rows, :] + acc_ref[rows, :] * scale_ref[rows, :] * half_gain
        _row_chunks(tm, body)


def _ffn(x, layer, pre_g, wg, wu, wd, post_g):
    n, d = x.shape
    f = wg.shape[2]
    tm, tf = FFN_TM, FFN_TF
    return pl.pallas_call(
        _ffn_kernel,
        grid=(n // tm, f // tf),
        in_specs=[
            pl.BlockSpec((tm, d), lambda i, j: (i, 0)),
            pl.BlockSpec((1, d), lambda i, j: (0, 0)),
            pl.BlockSpec((None, d, tf), lambda i, j: (layer, 0, j)),
            pl.BlockSpec((None, d, tf), lambda i, j: (layer, 0, j)),
            pl.BlockSpec((None, tf, d), lambda i, j: (layer, j, 0)),
            pl.BlockSpec((1, d), lambda i, j: (0, 0)),
        ],
        out_specs=pl.BlockSpec((tm, d), lambda i, j: (i, 0)),
        out_shape=jax.ShapeDtypeStruct((n, d), F32),
        scratch_shapes=[pltpu.VMEM((tm, d), BF16), pltpu.VMEM((tm, d), F32), pltpu.VMEM((tm, 1), F32)],
        compiler_params=_params("parallel", "arbitrary"),
        name="ffn",
    )(x, pre_g, wg, wu, wd, post_g)


def _proj_conv_kernel(xp_ref, x_ref, xn_ref, g_ref, w_ref, cw_ref, cb_ref, o_ref, *, tiles_per_seq):
    tm = x_ref.shape[0]
    pos = lax.rem(pl.program_id(0), tiles_per_seq)
    keep_prev = (pos != 0).astype(F32)
    keep_next = (pos != tiles_per_seq - 1).astype(F32)
    xa = jnp.concatenate([xp_ref[...] * keep_prev, x_ref[...], xn_ref[...] * keep_next], axis=0)
    xn = (_rms(xa) * g_ref[...]).astype(BF16)
    rows = xn.shape[0]
    mid = slice(SUBLANES, SUBLANES + tm)
    for part in range(o_ref.shape[0]):
        cols = slice(part * D_HYENA, (part + 1) * D_HYENA)
        p = _dot(xn, w_ref[:, cols])
        prev = pltpu.roll(p, 1, 0)[mid]
        nxt = pltpu.roll(p, rows - 1, 0)[mid]
        cw = cw_ref[:, cols]
        y = cb_ref[:, cols] + prev * cw[0:1] + p[mid] * cw[1:2] + nxt * cw[2:3]
        o_ref[part] = y.astype(o_ref.dtype)


def _proj_conv(x, layer, g, w_in, conv_w, conv_b, seq):
    n, d = x.shape
    parts = HYENA_ORDER + 1
    c = parts * D_HYENA
    tm = PROJ_TM
    halo_per_tile = tm // SUBLANES
    last_halo = n // SUBLANES - 1
    const2 = lambda i: (0, 0)
    return pl.pallas_call(
        functools.partial(_proj_conv_kernel, tiles_per_seq=seq // tm),
        grid=(n // tm,),
        in_specs=[
            pl.BlockSpec((SUBLANES, d), lambda i: (jnp.maximum(i * halo_per_tile - 1, 0), 0)),
            pl.BlockSpec((tm, d), lambda i: (i, 0)),
            pl.BlockSpec((SUBLANES, d), lambda i: (jnp.minimum((i + 1) * halo_per_tile, last_halo), 0)),
            pl.BlockSpec((1, d), const2),
            _resident((None, d, c), lambda i: (layer, 0, 0)),
            pl.BlockSpec((SHORT_CONV, c), const2),
            pl.BlockSpec((1, c), const2),
        ],
        out_specs=pl.BlockSpec((parts, tm, D_HYENA), lambda i: (0, i, 0)),
        out_shape=jax.ShapeDtypeStruct((parts, n, D_HYENA), BF16),
        compiler_params=_params("parallel"),
        name="hyena_proj_conv",
    )(x, x, x, g, w_in, conv_w, conv_b)


def _filter_kernel(z_ref, w1_ref, b_ref, freq_ref, wi_ref, wo_ref, delta_ref, o_ref):
    hp = functools.partial(jnp.dot, preferred_element_type=F32, precision=lax.Precision.HIGHEST)
    z = z_ref[...]
    b = b_ref[...]
    freq = freq_ref[...]
    h = jnp.sin(freq[0:1] * (hp(z, w1_ref[...]) + b[0:1]))
    for j in range(FILTER_INNER):
        h = jnp.sin(freq[j + 1:j + 2] * (hp(h, wi_ref[j]) + b[j + 1:j + 2]))
    h = _dot(h.astype(BF16), wo_ref[...].astype(BF16))
    t = z[:, 0:1]
    decay = jnp.exp(-t * delta_ref[...])
    tm = z.shape[0]
    row = pl.program_id(0) * tm + lax.broadcasted_iota(jnp.int32, (tm, D_HYENA), 0)
    for blk in range(HYENA_ORDER * N_DIRS):
        hb = h[:, blk * D_HYENA:(blk + 1) * D_HYENA] * decay
        if blk % N_DIRS == 1:
            hb = jnp.where(row == 0, 0.0, hb)
        o_ref[blk] = hb.astype(o_ref.dtype)


def _filter_taps(z, w1, b, freq, w_inner, w_out, abs_delta):
    seq = z.shape[0]
    nsig = HYENA_ORDER * N_DIRS
    tm = FILT_TM
    full = lambda a: pl.BlockSpec(a.shape, lambda i: (0,) * a.ndim)
    return pl.pallas_call(
        _filter_kernel,
        grid=(seq // tm,),
        in_specs=[pl.BlockSpec((tm, z.shape[1]), lambda i: (i, 0)),
                  full(w1), full(b), full(freq), full(w_inner), full(w_out), full(abs_delta)],
        out_specs=pl.BlockSpec((nsig, tm, D_HYENA), lambda i: (0, i, 0)),
        out_shape=jax.ShapeDtypeStruct((nsig, seq, D_HYENA), BF16),
        compiler_params=_params("parallel"),
        name="filter_taps",
    )(z, w1, b, freq, w_inner, w_out, abs_delta)


def _slab_groups(rows_per_block):
    return [slice(g * FFT_ROWS, (g + 1) * FFT_ROWS) for g in range(rows_per_block // FFT_ROWS)]


def _stage1_kernel(g_ref, x_ref, o_ref):
    h1, rb, c = x_ref.shape[1:]
    n1 = o_ref.shape[1]
    for rows in _slab_groups(rb):
        x = x_ref[0, :, rows, :].reshape(h1 * FFT_ROWS, c)
        o_ref[0, :, rows, :] = _dot(g_ref[...], x).astype(o_ref.dtype).reshape(n1, FFT_ROWS, c)


def _stage1(g1, x, part):
    _, nsig, h1, n2, c = x.shape
    r = FFT_ROWS * FFT_GROUPS
    n1 = g1.shape[0] // FFT_ROWS
    return pl.pallas_call(
        _stage1_kernel,
        grid=(nsig, n2 // r),
        in_specs=[
            pl.BlockSpec(g1.shape, lambda s, j: (0, 0)),
            pl.BlockSpec((None, 1, h1, r, c), lambda s, j: (part, s, 0, j, 0)),
        ],
        out_specs=pl.BlockSpec((1, n1, r, c), lambda s, j: (s, 0, j, 0)),
        out_shape=jax.ShapeDtypeStruct((nsig, n1, n2, c), BF16),
        compiler_params=_params("parallel", "parallel"),
        name="fft_stage1",
    )(g1, x)


def _filter_spectrum_kernel(t_ref, a_ref, k_ref, *, scale):
    t = t_ref[0]
    half = t.shape[0] // 2
    for o in range(HYENA_ORDER):
        xf = _dot(t, a_ref[N_DIRS * o, 0])
        xb = _dot(t, a_ref[N_DIRS * o + 1, 0])
        k_ref[o, 0, :half, :] = (xf[:half] + xb[:half]) * scale
        k_ref[o, 0, half:, :] = (xf[half:] - xb[half:]) * scale


def _filter_spectrum(t2, a, seq):
    nsig, h1, rows, c = a.shape
    return pl.pallas_call(
        functools.partial(_filter_spectrum_kernel, scale=1.0 / seq),
        grid=(h1,),
        in_specs=[
            pl.BlockSpec((1, rows, rows), lambda k: (k, 0, 0)),
            pl.BlockSpec((nsig, 1, rows, c), lambda k: (0, k, 0, 0)),
        ],
        out_specs=pl.BlockSpec((HYENA_ORDER, 1, rows, c), lambda k: (0, k, 0, 0)),
        out_shape=jax.ShapeDtypeStruct((HYENA_ORDER, h1, rows, c), F32),
        compiler_params=_params("parallel"),
        name="filter_spectrum",
    )(t2, a)


def _spectral_kernel(t_ref, tt_ref, a_ref, k_ref, o_ref):
    for q in range(t_ref.shape[0]):
        t = t_ref[q]
        tt = tt_ref[q]
        k = k_ref[0, q]
        half = k.shape[0] // 2
        kr, ki = k[:half], k[half:]
        for b in range(a_ref.shape[0]):
            x = _dot(t, a_ref[b, q])
            xr, xi = x[:half], x[half:]
            z = jnp.concatenate([xr * kr - xi * ki, xr * ki + xi * kr], axis=0).astype(BF16)
            o_ref[b, q] = _dot(tt, z).astype(o_ref.dtype)


def _spectral(t2, t2t, a, k, order):
    bsz, h1, rows, c = a.shape
    kb = SPECTRAL_K1
    return pl.pallas_call(
        _spectral_kernel,
        grid=(h1 // kb,),
        in_specs=[
            pl.BlockSpec((kb, rows, rows), lambda q: (q, 0, 0)),
            pl.BlockSpec((kb, rows, rows), lambda q: (q, 0, 0)),
            pl.BlockSpec((bsz, kb, rows, c), lambda q: (0, q, 0, 0)),
            pl.BlockSpec((1, kb, rows, c), lambda q: (order, q, 0, 0)),
        ],
        out_specs=pl.BlockSpec((bsz, kb, rows, c), lambda q: (0, q, 0, 0)),
        out_shape=jax.ShapeDtypeStruct((bsz, h1, rows, c), BF16),
        compiler_params=_params("parallel"),
        name="hyena_spectral",
    )(t2, t2t, a, k)


def _inverse_gate_kernel(gt_ref, b_ref, gate_ref, y_ref, skip_ref, *rest, emit_stage1):
    if emit_stage1:
        g_ref, o_ref, a_ref = rest
    else:
        (o_ref,), a_ref = rest, None
    n1, rb, c = b_ref.shape[1:]
    h1 = o_ref.shape[2]
    for rows in _slab_groups(rb):
        conv = _dot(gt_ref[...], b_ref[0, :, rows, :].reshape(n1 * FFT_ROWS, c))
        y = y_ref[0, :, rows, :].reshape(conv.shape).astype(F32)
        gate = gate_ref[0, :, rows, :].reshape(conv.shape).astype(F32)
        out = (gate * (conv + y * skip_ref[...])).astype(BF16)
        o_ref[0, 0, :, rows, :] = out.reshape(h1, FFT_ROWS, c)
        if a_ref is not None:
            a_ref[0, :, rows, :] = _dot(g_ref[...], out).astype(BF16).reshape(n1, FFT_ROWS, c)


def _inverse_gate(g1t, b, gate, gate_part, y, y_part, skip, g1=None):
    bsz, n1, n2, c = b.shape
    r = FFT_ROWS * FFT_GROUPS
    h1 = g1t.shape[0] // FFT_ROWS
    const2 = lambda s, j: (0, 0)
    in_specs = [
        pl.BlockSpec(g1t.shape, const2),
        pl.BlockSpec((1, n1, r, c), lambda s, j: (s, 0, j, 0)),
        pl.BlockSpec((None, 1, h1, r, c), lambda s, j: (gate_part, s, 0, j, 0)),
        pl.BlockSpec((None, 1, h1, r, c), lambda s, j: (y_part, s, 0, j, 0)),
        pl.BlockSpec((1, c), const2),
    ]
    out_specs = [pl.BlockSpec((1, 1, h1, r, c), lambda s, j: (0, s, 0, j, 0))]
    out_shape = [jax.ShapeDtypeStruct((1, bsz, h1, n2, c), BF16)]
    args = [g1t, b, gate, y, skip]
    if g1 is not None:
        in_specs.append(pl.BlockSpec(g1.shape, const2))
        out_specs.append(pl.BlockSpec((1, n1, r, c), lambda s, j: (s, 0, j, 0)))
        out_shape.append(jax.ShapeDtypeStruct((bsz, n1, n2, c), BF16))
        args.append(g1)
    return pl.pallas_call(
        functools.partial(_inverse_gate_kernel, emit_stage1=g1 is not None),
        grid=(bsz, n2 // r),
        in_specs=in_specs,
        out_specs=out_specs,
        out_shape=out_shape,
        compiler_params=_params("parallel", "parallel"),
        name="hyena_inverse_gate",
    )(*args)


def _sgu_kernel(x_ref, g_ref, wu_ref, wv_ref, wg0_ref, wg1_ref, lng_ref, lnb_ref, ws_ref, bs_ref,
                pb_ref, o_ref, s_ref):
    xn = (_rms(x_ref[...]) * g_ref[...]).astype(BF16)
    u = jax.nn.gelu(_dot(xn, wu_ref[...]), approximate=True)
    v = jax.nn.gelu(_dot(xn, wv_ref[...]), approximate=True)
    mu = jnp.mean(v, axis=-1, keepdims=True)
    vc = v - mu
    var = jnp.mean(vc * vc, axis=-1, keepdims=True)
    v = (vc * lax.rsqrt(var + LN_EPS) * lng_ref[...] + lnb_ref[...]).astype(BF16)
    dg = D_SGU // SGU_GROUPS
    for c in range(x_ref.shape[0] // SGU_CHUNK):
        rows = slice(c * SGU_CHUNK, (c + 1) * SGU_CHUNK)
        for grp in range(SGU_GROUPS):
            cols = slice(grp * dg, (grp + 1) * dg)
            s_ref[rows, cols] = _dot(ws_ref[grp], v[rows, cols]) + bs_ref[grp]
    gated = (u * s_ref[...]).astype(BF16)
    branch = _dot(gated, pb_ref[...])
    for h, wg_ref in enumerate((wg0_ref, wg1_ref)):
        cols = slice(h * W_IN_BLOCK, (h + 1) * W_IN_BLOCK)
        gate = jax.nn.sigmoid(_dot(xn, wg_ref[...]))
        o_ref[:, cols] = (gate * branch[:, cols]).astype(o_ref.dtype)


def _sgu(x, layer, g, w_in, ln_g, ln_b, w_s, b_s, p_b):
    n, d = x.shape
    tm = MIX_TM
    const2 = lambda i: (0, 0)
    first = (HYENA_ORDER + 1) * D_HYENA // W_IN_BLOCK
    gate_b = first + 2 * D_SGU // W_IN_BLOCK + d // W_IN_BLOCK
    w_blk = lambda blk: _resident((None, d, W_IN_BLOCK), lambda i: (layer, 0, blk))
    return pl.pallas_call(
        _sgu_kernel,
        grid=(n // tm,),
        in_specs=[
            pl.BlockSpec((tm, d), lambda i: (i, 0)),
            pl.BlockSpec((1, d), const2),
            w_blk(first), w_blk(first + 1), w_blk(gate_b), w_blk(gate_b + 1),
            pl.BlockSpec(ln_g.shape, const2),
            pl.BlockSpec(ln_b.shape, const2),
            _resident((None,) + w_s.shape[1:], lambda i: (layer, 0, 0, 0)),
            _resident(b_s.shape, lambda i: (0, 0, 0)),
            _resident((None,) + p_b.shape[1:], lambda i: (layer, 0, 0)),
        ],
        out_specs=pl.BlockSpec((tm, d), lambda i: (i, 0)),
        out_shape=jax.ShapeDtypeStruct((n, d), BF16),
        scratch_shapes=[pltpu.VMEM((tm, D_SGU), F32)],
        compiler_params=_params("parallel"),
        name="sgu_branch",
    )(x, g, w_in, w_in, w_in, w_in, ln_g, ln_b, w_s, b_s, p_b)


def _merge_kernel(x_ref, g_ref, wg0_ref, wg1_ref, a_ref, pa_ref, mb_ref, wout_ref, post_g_ref, o_ref):
    x = x_ref[...]
    xn = (_rms(x) * g_ref[...]).astype(BF16)
    pa = _dot(a_ref[...], pa_ref[...])
    halves = []
    for h, wg_ref in enumerate((wg0_ref, wg1_ref)):
        cols = slice(h * W_IN_BLOCK, (h + 1) * W_IN_BLOCK)
        gate = jax.nn.sigmoid(_dot(xn, wg_ref[...]))
        halves.append((gate * pa[:, cols] + mb_ref[:, cols].astype(F32)).astype(BF16))
    y = _dot(jnp.concatenate(halves, axis=1), wout_ref[...])
    o_ref[...] = x + _rms(y) * post_g_ref[...]


def _merge(x, layer, g, w_in, a, p_a, mb, w_out, post_g):
    n, d = x.shape
    tm = MIX_TM
    const2 = lambda i: (0, 0)
    gate_a = ((HYENA_ORDER + 1) * D_HYENA + 2 * D_SGU) // W_IN_BLOCK
    w_blk = lambda blk: _resident((None, d, W_IN_BLOCK), lambda i: (layer, 0, blk))
    return pl.pallas_call(
        _merge_kernel,
        grid=(n // tm,),
        in_specs=[
            pl.BlockSpec((tm, d), lambda i: (i, 0)),
            pl.BlockSpec((1, d), const2),
            w_blk(gate_a), w_blk(gate_a + 1),
            pl.BlockSpec((tm, a.shape[1]), lambda i: (i, 0)),
            _resident((None,) + p_a.shape[1:], lambda i: (layer, 0, 0)),
            pl.BlockSpec((tm, d), lambda i: (i, 0)),
            _resident((None,) + w_out.shape[1:], lambda i: (layer, 0, 0)),
            pl.BlockSpec((1, d), const2),
        ],
        out_specs=pl.BlockSpec((tm, d), lambda i: (i, 0)),
        out_shape=jax.ShapeDtypeStruct((n, d), F32),
        compiler_params=_params("parallel"),
        name="merge",
    )(x, g, w_in, w_in, a, p_a, mb, w_out, post_g)


def _filter_positions(seq):
    t = jnp.linspace(0.0, 1.0, seq, dtype=F32)[:, None]
    w = (2.0 * math.pi / seq) * jnp.arange(seq, dtype=F32)[:, None]
    f = jnp.linspace(1e-4, FILTER_BANDS - 1, FILTER_BANDS, dtype=F32)[None, :]
    z = jnp.concatenate([t, jnp.cos(f * w), -jnp.sin(f * w)], axis=-1)
    return jnp.pad(z, ((0, 0), (0, LANES - FILTER_EMB)))


def _decay_rates():
    max_decay = math.log(DECAY_TARGET) / DECAY_FAST_PCT
    min_decay = math.log(DECAY_TARGET) / DECAY_SLOW_PCT
    return jnp.abs(jnp.linspace(min_decay, max_decay, D_HYENA, dtype=F32))[None, :]


def _fft_tables(seq):
    n = 2 * seq
    n2 = FFT_N2
    n1 = n // n2
    h1 = n1 // 2
    k1 = jnp.arange(h1, dtype=jnp.int32)
    m1 = ((2 * k1[:, None] + 1) * k1[None, :]) % (2 * n1)
    a1 = m1.astype(F32) * (math.pi / n1)
    f1 = jnp.stack([jnp.cos(a1), -jnp.sin(a1)], axis=1).reshape(n1, h1)
    q = jnp.arange(n2, dtype=jnp.int32)
    kk = 2 * (k1[:, None, None] + n1 * q[None, :, None]) + 1
    a2 = ((kk * q[None, None, :]) % (2 * n)).astype(F32) * (math.pi / n)
    c, s = jnp.cos(a2), jnp.sin(a2)
    t2 = jnp.concatenate([jnp.concatenate([c, s], axis=2), jnp.concatenate([-s, c], axis=2)], axis=1)
    eye = jnp.eye(FFT_ROWS, dtype=F32)
    g1, g1t = jnp.kron(f1, eye), jnp.kron(f1.T, eye)
    return (g1.astype(BF16), g1t.astype(BF16), t2.astype(BF16), jnp.swapaxes(t2, 1, 2).astype(BF16))


def kernel(x, ffn1_pre_g, ffn1_w_gate, ffn1_w_up, ffn1_w_down, ffn1_post_g, mix_pre_g, w_in, hy_conv_w, hy_conv_b, filt_w1, filt_b, filt_freq, filt_w_inner, filt_w_out, hy_skip, sgu_ln_g, sgu_ln_b, sgu_w_s, sgu_b_s, p_a, p_b, w_out, mix_post_g, ffn2_pre_g, ffn2_w_gate, ffn2_w_up, ffn2_w_down, ffn2_post_g):
    bsz, seq, d = x.shape
    depth = w_in.shape[0]
    n_tok = bsz * seq

    z = _filter_positions(seq)
    abs_delta = _decay_rates()
    g1, g1t, t2, t2t = _fft_tables(seq)
    n2 = FFT_N2
    h1 = seq // n2
    n1 = 2 * h1
    nsig = HYENA_ORDER * N_DIRS
    row = lambda v: v[None, :]
    bf = lambda w: w.astype(BF16)

    ffn1 = (bf(ffn1_w_gate), bf(ffn1_w_up), bf(ffn1_w_down))
    ffn2 = (bf(ffn2_w_gate), bf(ffn2_w_up), bf(ffn2_w_down))
    w_in, p_a, p_b, w_out, sgu_w_s = bf(w_in), bf(p_a), bf(p_b), bf(w_out), bf(sgu_w_s)

    xs = x.reshape(n_tok, d)
    for i in range(depth):
        xs = _ffn(xs, i, row(ffn1_pre_g[i]), *ffn1, row(ffn1_post_g[i]))
        pre_g = row(mix_pre_g[i])

        u = _proj_conv(xs, i, pre_g, w_in, hy_conv_w[i], row(hy_conv_b[i]), seq)
        u = u.reshape(HYENA_ORDER + 1, bsz, h1, n2, D_HYENA)
        w1 = jnp.pad(filt_w1[i], ((0, LANES - FILTER_EMB), (0, 0)))
        taps = _filter_taps(z, w1, filt_b[i], filt_freq[i], filt_w_inner[i], filt_w_out[i], abs_delta)
        taps_a = _stage1(g1, taps.reshape(1, nsig, h1, n2, D_HYENA), 0)
        k_spec = _filter_spectrum(t2, taps_a.reshape(nsig, h1, 2 * n2, D_HYENA), seq)
        y, y_part = u, 0
        a1 = _stage1(g1, y, y_part)
        for o in range(HYENA_ORDER):
            last = o == HYENA_ORDER - 1
            b2 = _spectral(t2, t2t, a1.reshape(bsz, h1, 2 * n2, D_HYENA), k_spec, o)
            outs = _inverse_gate(g1t, b2.reshape(bsz, n1, n2, D_HYENA), u, o + 1, y, y_part,
                                 row(hy_skip[i, o]), g1=None if last else g1)
            y, y_part = outs[0], 0
            a1 = None if last else outs[1]
        a = y.reshape(n_tok, D_HYENA)

        b_s = jnp.broadcast_to(sgu_b_s[i][:, :, None], (SGU_GROUPS, SGU_CHUNK, D_SGU // SGU_GROUPS))
        mb = _sgu(xs, i, pre_g, w_in, row(sgu_ln_g[i]), row(sgu_ln_b[i]), sgu_w_s, b_s, p_b)

        xs = _merge(xs, i, pre_g, w_in, a, p_a, mb, w_out, row(mix_post_g[i]))
        xs = _ffn(xs, i, row(ffn2_pre_g[i]), *ffn2, row(ffn2_post_g[i]))
    return xs.reshape(bsz, seq, d)
```

```python
import functools
import math

import jax
import jax.numpy as jnp
from jax import lax
from jax.experimental import pallas as pl
from jax.experimental.pallas import tpu as pltpu

F32 = jnp.float32
BF16 = jnp.bfloat16

D_MODEL = 2048
D_FF = 5632
D_HYENA = D_MODEL // 2
HYENA_ORDER = 2
SHORT_CONV = 3
FILTER_BANDS = 16
FILTER_EMB = 1 + 2 * FILTER_BANDS
FILTER_HIDDEN = 64
FILTER_INNER = 2
DECAY_FAST_PCT = 0.3
DECAY_SLOW_PCT = 1.5
DECAY_TARGET = 1e-2
N_DIRS = 2
D_SGU = D_MODEL // 2
SGU_CHUNK = 128
SGU_GROUPS = 8
NORM_EPS = 1e-6
LN_EPS = 1e-5

LANES = 128
SUBLANES = 8
VMEM_LIMIT_BYTES = 56 * 1024 * 1024
FFN_VMEM_LIMIT_BYTES = 61 * 1024 * 1024

FFN_TM = 1024
FFN_TF = 256
PROJ_TM = 512
MIX_TM = 512
FILT_TM = 512
FFT_N2 = 128
FFT_ROWS = 16
FFT_GROUPS = 2
SPECTRAL_K1 = 2
ROW_CHUNK = 16
ROW_UNROLL = True
W_IN_BLOCK = 1024


def _params(*sem, vmem=VMEM_LIMIT_BYTES):
    return pltpu.CompilerParams(dimension_semantics=sem, vmem_limit_bytes=vmem)


def _resident(shape, index_map):
    return pl.BlockSpec(shape, index_map, pipeline_mode=pl.Buffered(1))


def _rms(x):
    return x * lax.rsqrt(jnp.mean(x * x, axis=-1, keepdims=True) + NORM_EPS)


def _dot(a, b):
    return jnp.dot(a, b, preferred_element_type=F32)


def _row_chunks(n_rows, body, unroll=ROW_UNROLL):
    def step(r, carry):
        body(pl.ds(pl.multiple_of(r * ROW_CHUNK, ROW_CHUNK), ROW_CHUNK))
        return carry
    lax.fori_loop(0, n_rows // ROW_CHUNK, step, 0, unroll=unroll)


def _rms_scales(src_ref, scale_ref):
    def body(rows):
        a = src_ref[rows, :]
        scale_ref[rows, :] = lax.rsqrt(jnp.mean(a * a, axis=-1, keepdims=True) + NORM_EPS)
    _row_chunks(src_ref.shape[0], body)


def _ffn_kernel(x_ref, pre_g_ref, wg_ref, wu_ref, wd_ref, post_g_ref, o_ref, xn_ref, acc_ref, scale_ref):
    j = pl.program_id(1)
    tm = x_ref.shape[0]

    @pl.when(j == 0)
    def _():
        _rms_scales(x_ref, scale_ref)
        gain = pre_g_ref[...]

        def body(rows):
            xn_ref[rows, :] = (x_ref[rows, :] * scale_ref[rows, :] * gain).astype(BF16)
        _row_chunks(tm, body)

    def step(first):
        xn = xn_ref[...]
        g = _dot(xn, wg_ref[...].astype(BF16))
        u = _dot(xn, wu_ref[...].astype(BF16))
        h = (g * jax.nn.sigmoid(g) * u).astype(BF16)
        d = _dot(h, wd_ref[...].astype(BF16))
        acc_ref[...] = d if first else acc_ref[...] + d

    pl.when(j == 0)(functools.partial(step, True))
    pl.when(j > 0)(functools.partial(step, False))

    @pl.when(j == pl.num_programs(1) - 1)
    def _():
        _rms_scales(acc_ref, scale_ref)
        half_gain = 0.5 * post_g_ref[...]

        def body(rows):
            o_ref[rows, :] = x_ref[rows, :] + acc_ref[rows, :] * scale_ref[rows, :] * half_gain
        _row_chunks(tm, body)


def _ffn(x, layer, pre_g, wg, wu, wd, post_g):
    n, d = x.shape
    f = wg.shape[2]
    tm, tf = FFN_TM, FFN_TF
    return pl.pallas_call(
        _ffn_kernel,
        grid=(n // tm, f // tf),
        in_specs=[
            pl.BlockSpec((tm, d), lambda i, j: (i, 0)),
            pl.BlockSpec((1, d), lambda i, j: (0, 0)),
            pl.BlockSpec((None, d, tf), lambda i, j: (layer, 0, j)),
            pl.BlockSpec((None, d, tf), lambda i, j: (layer, 0, j)),
            pl.BlockSpec((None, tf, d), lambda i, j: (layer, j, 0)),
            pl.BlockSpec((1, d), lambda i, j: (0, 0)),
        ],
        out_specs=pl.BlockSpec((tm, d), lambda i, j: (i, 0)),
        out_shape=jax.ShapeDtypeStruct((n, d), F32),
        scratch_shapes=[pltpu.VMEM((tm, d), BF16), pltpu.VMEM((tm, d), F32), pltpu.VMEM((tm, 1), F32)],
        compiler_params=_params("parallel", "arbitrary", vmem=FFN_VMEM_LIMIT_BYTES),
        name="ffn",
    )(x, pre_g, wg, wu, wd, post_g)


def _proj_conv_kernel(xp_ref, x_ref, xn_ref, g_ref, w_ref, cw_ref, cb_ref, o_ref, *, tiles_per_seq):
    tm = x_ref.shape[0]
    pos = lax.rem(pl.program_id(0), tiles_per_seq)
    keep_prev = (pos != 0).astype(F32)
    keep_next = (pos != tiles_per_seq - 1).astype(F32)
    xa = jnp.concatenate([xp_ref[...] * keep_prev, x_ref[...], xn_ref[...] * keep_next], axis=0)
    xn = (_rms(xa) * g_ref[...]).astype(BF16)
    rows = xn.shape[0]
    mid = slice(SUBLANES, SUBLANES + tm)
    for part in range(o_ref.shape[0]):
        cols = slice(part * D_HYENA, (part + 1) * D_HYENA)
        p = _dot(xn, w_ref[:, cols])
        prev = pltpu.roll(p, 1, 0)[mid]
        nxt = pltpu.roll(p, rows - 1, 0)[mid]
        cw = cw_ref[:, cols]
        y = cb_ref[:, cols] + prev * cw[0:1] + p[mid] * cw[1:2] + nxt * cw[2:3]
        o_ref[part] = y.astype(o_ref.dtype)


def _proj_conv(x, layer, g, w_in, conv_w, conv_b, seq):
    n, d = x.shape
    parts = HYENA_ORDER + 1
    c = parts * D_HYENA
    tm = PROJ_TM
    halo_per_tile = tm // SUBLANES
    last_halo = n // SUBLANES - 1
    const2 = lambda i: (0, 0)
    return pl.pallas_call(
        functools.partial(_proj_conv_kernel, tiles_per_seq=seq // tm),
        grid=(n // tm,),
        in_specs=[
            pl.BlockSpec((SUBLANES, d), lambda i: (jnp.maximum(i * halo_per_tile - 1, 0), 0)),
            pl.BlockSpec((tm, d), lambda i: (i, 0)),
            pl.BlockSpec((SUBLANES, d), lambda i: (jnp.minimum((i + 1) * halo_per_tile, last_halo), 0)),
            pl.BlockSpec((1, d), const2),
            _resident((None, d, c), lambda i: (layer, 0, 0)),
            pl.BlockSpec((SHORT_CONV, c), const2),
            pl.BlockSpec((1, c), const2),
        ],
        out_specs=pl.BlockSpec((parts, tm, D_HYENA), lambda i: (0, i, 0)),
        out_shape=jax.ShapeDtypeStruct((parts, n, D_HYENA), BF16),
        compiler_params=_params("parallel"),
        name="hyena_proj_conv",
    )(x, x, x, g, w_in, conv_w, conv_b)


def _filter_kernel(z_ref, w1_ref, b_ref, freq_ref, wi_ref, wo_ref, delta_ref, o_ref):
    hp = functools.partial(jnp.dot, preferred_element_type=F32, precision=lax.Precision.HIGHEST)
    z = z_ref[...]
    b = b_ref[...]
    freq = freq_ref[...]
    h = jnp.sin(freq[0:1] * (hp(z, w1_ref[...]) + b[0:1]))
    for j in range(FILTER_INNER):
        h = jnp.sin(freq[j + 1:j + 2] * (hp(h, wi_ref[j]) + b[j + 1:j + 2]))
    h = _dot(h.astype(BF16), wo_ref[...].astype(BF16))
    t = z[:, 0:1]
    decay = jnp.exp(-t * delta_ref[...])
    tm = z.shape[0]
    row = pl.program_id(0) * tm + lax.broadcasted_iota(jnp.int32, (tm, D_HYENA), 0)
    for blk in range(HYENA_ORDER * N_DIRS):
        hb = h[:, blk * D_HYENA:(blk + 1) * D_HYENA] * decay
        if blk % N_DIRS == 1:
            hb = jnp.where(row == 0, 0.0, hb)
        o_ref[blk] = hb.astype(o_ref.dtype)


def _filter_taps(z, w1, b, freq, w_inner, w_out, abs_delta):
    seq = z.shape[0]
    nsig = HYENA_ORDER * N_DIRS
    tm = FILT_TM
    full = lambda a: pl.BlockSpec(a.shape, lambda i: (0,) * a.ndim)
    return pl.pallas_call(
        _filter_kernel,
        grid=(seq // tm,),
        in_specs=[pl.BlockSpec((tm, z.shape[1]), lambda i: (i, 0)),
                  full(w1), full(b), full(freq), full(w_inner), full(w_out), full(abs_delta)],
        out_specs=pl.BlockSpec((nsig, tm, D_HYENA), lambda i: (0, i, 0)),
        out_shape=jax.ShapeDtypeStruct((nsig, seq, D_HYENA), BF16),
        compiler_params=_params("parallel"),
        name="filter_taps",
    )(z, w1, b, freq, w_inner, w_out, abs_delta)


def _slab_groups(rows_per_block):
    return [slice(g * FFT_ROWS, (g + 1) * FFT_ROWS) for g in range(rows_per_block // FFT_ROWS)]


def _stage1_kernel(g_ref, x_ref, o_ref):
    h1, rb, c = x_ref.shape[1:]
    n1 = o_ref.shape[1]
    for rows in _slab_groups(rb):
        x = x_ref[0, :, rows, :].reshape(h1 * FFT_ROWS, c)
        o_ref[0, :, rows, :] = _dot(g_ref[...], x).astype(o_ref.dtype).reshape(n1, FFT_ROWS, c)


def _stage1(g1, x, part):
    _, nsig, h1, n2, c = x.shape
    r = FFT_ROWS * FFT_GROUPS
    n1 = g1.shape[0] // FFT_ROWS
    return pl.pallas_call(
        _stage1_kernel,
        grid=(nsig, n2 // r),
        in_specs=[
            pl.BlockSpec(g1.shape, lambda s, j: (0, 0)),
            pl.BlockSpec((None, 1, h1, r, c), lambda s, j: (part, s, 0, j, 0)),
        ],
        out_specs=pl.BlockSpec((1, n1, r, c), lambda s, j: (s, 0, j, 0)),
        out_shape=jax.ShapeDtypeStruct((nsig, n1, n2, c), BF16),
        compiler_params=_params("parallel", "parallel"),
        name="fft_stage1",
    )(g1, x)


def _filter_spectrum_kernel(t_ref, a_ref, k_ref, *, scale):
    t = t_ref[0]
    half = t.shape[0] // 2
    for o in range(HYENA_ORDER):
        xf = _dot(t, a_ref[N_DIRS * o, 0])
        xb = _dot(t, a_ref[N_DIRS * o + 1, 0])
        k_ref[o, 0, :half, :] = (xf[:half] + xb[:half]) * scale
        k_ref[o, 0, half:, :] = (xf[half:] - xb[half:]) * scale


def _filter_spectrum(t2, a, seq):
    nsig, h1, rows, c = a.shape
    return pl.pallas_call(
        functools.partial(_filter_spectrum_kernel, scale=1.0 / seq),
        grid=(h1,),
        in_specs=[
            pl.BlockSpec((1, rows, rows), lambda k: (k, 0, 0)),
            pl.BlockSpec((nsig, 1, rows, c), lambda k: (0, k, 0, 0)),
        ],
        out_specs=pl.BlockSpec((HYENA_ORDER, 1, rows, c), lambda k: (0, k, 0, 0)),
        out_shape=jax.ShapeDtypeStruct((HYENA_ORDER, h1, rows, c), F32),
        compiler_params=_params("parallel"),
        name="filter_spectrum",
    )(t2, a)


def _spectral_kernel(t_ref, tt_ref, a_ref, k_ref, o_ref):
    for q in range(t_ref.shape[0]):
        t = t_ref[q]
        tt = tt_ref[q]
        k = k_ref[0, q]
        half = k.shape[0] // 2
        kr, ki = k[:half], k[half:]
        for b in range(a_ref.shape[0]):
            x = _dot(t, a_ref[b, q])
            xr, xi = x[:half], x[half:]
            z = jnp.concatenate([xr * kr - xi * ki, xr * ki + xi * kr], axis=0).astype(BF16)
            o_ref[b, q] = _dot(tt, z).astype(o_ref.dtype)


def _spectral(t2, t2t, a, k, order):
    bsz, h1, rows, c = a.shape
    kb = SPECTRAL_K1
    return pl.pallas_call(
        _spectral_kernel,
        grid=(h1 // kb,),
        in_specs=[
            pl.BlockSpec((kb, rows, rows), lambda q: (q, 0, 0)),
            pl.BlockSpec((kb, rows, rows), lambda q: (q, 0, 0)),
            pl.BlockSpec((bsz, kb, rows, c), lambda q: (0, q, 0, 0)),
            pl.BlockSpec((1, kb, rows, c), lambda q: (order, q, 0, 0)),
        ],
        out_specs=pl.BlockSpec((bsz, kb, rows, c), lambda q: (0, q, 0, 0)),
        out_shape=jax.ShapeDtypeStruct((bsz, h1, rows, c), BF16),
        compiler_params=_params("parallel"),
        name="hyena_spectral",
    )(t2, t2t, a, k)


def _inverse_gate_kernel(gt_ref, b_ref, gate_ref, y_ref, skip_ref, *rest, emit_stage1):
    if emit_stage1:
        g_ref, o_ref, a_ref = rest
    else:
        (o_ref,), a_ref = rest, None
    n1, rb, c = b_ref.shape[1:]
    h1 = o_ref.shape[2]
    for rows in _slab_groups(rb):
        conv = _dot(gt_ref[...], b_ref[0, :, rows, :].reshape(n1 * FFT_ROWS, c))
        y = y_ref[0, :, rows, :].reshape(conv.shape).astype(F32)
        gate = gate_ref[0, :, rows, :].reshape(conv.shape).astype(F32)
        out = (gate * (conv + y * skip_ref[...])).astype(BF16)
        o_ref[0, 0, :, rows, :] = out.reshape(h1, FFT_ROWS, c)
        if a_ref is not None:
            a_ref[0, :, rows, :] = _dot(g_ref[...], out).astype(BF16).reshape(n1, FFT_ROWS, c)


def _inverse_gate(g1t, b, gate, gate_part, y, y_part, skip, g1=None):
    bsz, n1, n2, c = b.shape
    r = FFT_ROWS * FFT_GROUPS
    h1 = g1t.shape[0] // FFT_ROWS
    const2 = lambda s, j: (0, 0)
    in_specs = [
        pl.BlockSpec(g1t.shape, const2),
        pl.BlockSpec((1, n1, r, c), lambda s, j: (s, 0, j, 0)),
        pl.BlockSpec((None, 1, h1, r, c), lambda s, j: (gate_part, s, 0, j, 0)),
        pl.BlockSpec((None, 1, h1, r, c), lambda s, j: (y_part, s, 0, j, 0)),
        pl.BlockSpec((1, c), const2),
    ]
    out_specs = [pl.BlockSpec((1, 1, h1, r, c), lambda s, j: (0, s, 0, j, 0))]
    out_shape = [jax.ShapeDtypeStruct((1, bsz, h1, n2, c), BF16)]
    args = [g1t, b, gate, y, skip]
    if g1 is not None:
        in_specs.append(pl.BlockSpec(g1.shape, const2))
        out_specs.append(pl.BlockSpec((1, n1, r, c), lambda s, j: (s, 0, j, 0)))
        out_shape.append(jax.ShapeDtypeStruct((bsz, n1, n2, c), BF16))
        args.append(g1)
    return pl.pallas_call(
        functools.partial(_inverse_gate_kernel, emit_stage1=g1 is not None),
        grid=(bsz, n2 // r),
        in_specs=in_specs,
        out_specs=out_specs,
        out_shape=out_shape,
        compiler_params=_params("parallel", "parallel"),
        name="hyena_inverse_gate",
    )(*args)


def _sgu_kernel(x_ref, g_ref, wu_ref, wv_ref, wg0_ref, wg1_ref, lng_ref, lnb_ref, ws_ref, bs_ref,
                pb_ref, o_ref, s_ref):
    xn = (_rms(x_ref[...]) * g_ref[...]).astype(BF16)
    u = jax.nn.gelu(_dot(xn, wu_ref[...]), approximate=True)
    v = jax.nn.gelu(_dot(xn, wv_ref[...]), approximate=True)
    mu = jnp.mean(v, axis=-1, keepdims=True)
    vc = v - mu
    var = jnp.mean(vc * vc, axis=-1, keepdims=True)
    v = (vc * lax.rsqrt(var + LN_EPS) * lng_ref[...] + lnb_ref[...]).astype(BF16)
    dg = D_SGU // SGU_GROUPS
    for c in range(x_ref.shape[0] // SGU_CHUNK):
        rows = slice(c * SGU_CHUNK, (c + 1) * SGU_CHUNK)
        for grp in range(SGU_GROUPS):
            cols = slice(grp * dg, (grp + 1) * dg)
            s_ref[rows, cols] = _dot(ws_ref[grp], v[rows, cols]) + bs_ref[grp]
    gated = (u * s_ref[...]).astype(BF16)
    branch = _dot(gated, pb_ref[...])
    for h, wg_ref in enumerate((wg0_ref, wg1_ref)):
        cols = slice(h * W_IN_BLOCK, (h + 1) * W_IN_BLOCK)
        gate = jax.nn.sigmoid(_dot(xn, wg_ref[...]))
        o_ref[:, cols] = (gate * branch[:, cols]).astype(o_ref.dtype)


def _sgu(x, layer, g, w_in, ln_g, ln_b, w_s, b_s, p_b):
    n, d = x.shape
    tm = MIX_TM
    const2 = lambda i: (0, 0)
    first = (HYENA_ORDER + 1) * D_HYENA // W_IN_BLOCK
    gate_b = first + 2 * D_SGU // W_IN_BLOCK + d // W_IN_BLOCK
    w_blk = lambda blk: _resident((None, d, W_IN_BLOCK), lambda i: (layer, 0, blk))
    return pl.pallas_call(
        _sgu_kernel,
        grid=(n // tm,),
        in_specs=[
            pl.BlockSpec((tm, d), lambda i: (i, 0)),
            pl.BlockSpec((1, d), const2),
            w_blk(first), w_blk(first + 1), w_blk(gate_b), w_blk(gate_b + 1),
            pl.BlockSpec(ln_g.shape, const2),
            pl.BlockSpec(ln_b.shape, const2),
            _resident((None,) + w_s.shape[1:], lambda i: (layer, 0, 0, 0)),
            _resident(b_s.shape, lambda i: (0, 0, 0)),
            _resident((None,) + p_b.shape[1:], lambda i: (layer, 0, 0)),
        ],
        out_specs=pl.BlockSpec((tm, d), lambda i: (i, 0)),
        out_shape=jax.ShapeDtypeStruct((n, d), BF16),
        scratch_shapes=[pltpu.VMEM((tm, D_SGU), F32)],
        compiler_params=_params("parallel"),
        name="sgu_branch",
    )(x, g, w_in, w_in, w_in, w_in, ln_g, ln_b, w_s, b_s, p_b)


def _merge_kernel(x_ref, g_ref, wg0_ref, wg1_ref, a_ref, pa_ref, mb_ref, wout_ref, post_g_ref, o_ref):
    x = x_ref[...]
    xn = (_rms(x) * g_ref[...]).astype(BF16)
    pa = _dot(a_ref[...], pa_ref[...])
    halves = []
    for h, wg_ref in enumerate((wg0_ref, wg1_ref)):
        cols = slice(h * W_IN_BLOCK, (h + 1) * W_IN_BLOCK)
        gate = jax.nn.sigmoid(_dot(xn, wg_ref[...]))
        halves.append((gate * pa[:, cols] + mb_ref[:, cols].astype(F32)).astype(BF16))
    y = _dot(jnp.concatenate(halves, axis=1), wout_ref[...])
    o_ref[...] = x + _rms(y) * post_g_ref[...]


def _merge(x, layer, g, w_in, a, p_a, mb, w_out, post_g):
    n, d = x.shape
    tm = MIX_TM
    const2 = lambda i: (0, 0)
    gate_a = ((HYENA_ORDER + 1) * D_HYENA + 2 * D_SGU) // W_IN_BLOCK
    w_blk = lambda blk: _resident((None, d, W_IN_BLOCK), lambda i: (layer, 0, blk))
    return pl.pallas_call(
        _merge_kernel,
        grid=(n // tm,),
        in_specs=[
            pl.BlockSpec((tm, d), lambda i: (i, 0)),
            pl.BlockSpec((1, d), const2),
            w_blk(gate_a), w_blk(gate_a + 1),
            pl.BlockSpec((tm, a.shape[1]), lambda i: (i, 0)),
            _resident((None,) + p_a.shape[1:], lambda i: (layer, 0, 0)),
            pl.BlockSpec((tm, d), lambda i: (i, 0)),
            _resident((None,) + w_out.shape[1:], lambda i: (layer, 0, 0)),
            pl.BlockSpec((1, d), const2),
        ],
        out_specs=pl.BlockSpec((tm, d), lambda i: (i, 0)),
        out_shape=jax.ShapeDtypeStruct((n, d), F32),
        compiler_params=_params("parallel"),
        name="merge",
    )(x, g, w_in, w_in, a, p_a, mb, w_out, post_g)


def _filter_positions(seq):
    t = jnp.linspace(0.0, 1.0, seq, dtype=F32)[:, None]
    w = (2.0 * math.pi / seq) * jnp.arange(seq, dtype=F32)[:, None]
    f = jnp.linspace(1e-4, FILTER_BANDS - 1, FILTER_BANDS, dtype=F32)[None, :]
    z = jnp.concatenate([t, jnp.cos(f * w), -jnp.sin(f * w)], axis=-1)
    return jnp.pad(z, ((0, 0), (0, LANES - FILTER_EMB)))


def _decay_rates():
    max_decay = math.log(DECAY_TARGET) / DECAY_FAST_PCT
    min_decay = math.log(DECAY_TARGET) / DECAY_SLOW_PCT
    return jnp.abs(jnp.linspace(min_decay, max_decay, D_HYENA, dtype=F32))[None, :]


def _fft_tables(seq):
    n = 2 * seq
    n2 = FFT_N2
    n1 = n // n2
    h1 = n1 // 2
    k1 = jnp.arange(h1, dtype=jnp.int32)
    m1 = ((2 * k1[:, None] + 1) * k1[None, :]) % (2 * n1)
    a1 = m1.astype(F32) * (math.pi / n1)
    f1 = jnp.stack([jnp.cos(a1), -jnp.sin(a1)], axis=1).reshape(n1, h1)
    q = jnp.arange(n2, dtype=jnp.int32)
    kk = 2 * (k1[:, None, None] + n1 * q[None, :, None]) + 1
    a2 = ((kk * q[None, None, :]) % (2 * n)).astype(F32) * (math.pi / n)
    c, s = jnp.cos(a2), jnp.sin(a2)
    t2 = jnp.concatenate([jnp.concatenate([c, s], axis=2), jnp.concatenate([-s, c], axis=2)], axis=1)
    eye = jnp.eye(FFT_ROWS, dtype=F32)
    g1, g1t = jnp.kron(f1, eye), jnp.kron(f1.T, eye)
    return (g1.astype(BF16), g1t.astype(BF16), t2.astype(BF16), jnp.swapaxes(t2, 1, 2).astype(BF16))


def kernel(x, ffn1_pre_g, ffn1_w_gate, ffn1_w_up, ffn1_w_down, ffn1_post_g, mix_pre_g, w_in, hy_conv_w, hy_conv_b, filt_w1, filt_b, filt_freq, filt_w_inner, filt_w_out, hy_skip, sgu_ln_g, sgu_ln_b, sgu_w_s, sgu_b_s, p_a, p_b, w_out, mix_post_g, ffn2_pre_g, ffn2_w_gate, ffn2_w_up, ffn2_w_down, ffn2_post_g):
    bsz, seq, d = x.shape
    depth = w_in.shape[0]
    n_tok = bsz * seq

    z = _filter_positions(seq)
    abs_delta = _decay_rates()
    g1, g1t, t2, t2t = _fft_tables(seq)
    n2 = FFT_N2
    h1 = seq // n2
    n1 = 2 * h1
    nsig = HYENA_ORDER * N_DIRS
    row = lambda v: v[None, :]
    bf = lambda w: w.astype(BF16)

    ffn1 = (ffn1_w_gate, ffn1_w_up, ffn1_w_down)
    ffn2 = (ffn2_w_gate, ffn2_w_up, ffn2_w_down)
    w_in, p_a, p_b, w_out, sgu_w_s = bf(w_in), bf(p_a), bf(p_b), bf(w_out), bf(sgu_w_s)

    xs = x.reshape(n_tok, d)
    for i in range(depth):
        xs = _ffn(xs, i, row(ffn1_pre_g[i]), *ffn1, row(ffn1_post_g[i]))
        pre_g = row(mix_pre_g[i])

        u = _proj_conv(xs, i, pre_g, w_in, hy_conv_w[i], row(hy_conv_b[i]), seq)
        u = u.reshape(HYENA_ORDER + 1, bsz, h1, n2, D_HYENA)
        w1 = jnp.pad(filt_w1[i], ((0, LANES - FILTER_EMB), (0, 0)))
        taps = _filter_taps(z, w1, filt_b[i], filt_freq[i], filt_w_inner[i], filt_w_out[i], abs_delta)
        taps_a = _stage1(g1, taps.reshape(1, nsig, h1, n2, D_HYENA), 0)
        k_spec = _filter_spectrum(t2, taps_a.reshape(nsig, h1, 2 * n2, D_HYENA), seq)
        y, y_part = u, 0
        a1 = _stage1(g1, y, y_part)
        for o in range(HYENA_ORDER):
            last = o == HYENA_ORDER - 1
            b2 = _spectral(t2, t2t, a1.reshape(bsz, h1, 2 * n2, D_HYENA), k_spec, o)
            outs = _inverse_gate(g1t, b2.reshape(bsz, n1, n2, D_HYENA), u, o + 1, y, y_part,
                                 row(hy_skip[i, o]), g1=None if last else g1)
            y, y_part = outs[0], 0
            a1 = None if last else outs[1]
        a = y.reshape(n_tok, D_HYENA)

        b_s = jnp.broadcast_to(sgu_b_s[i][:, :, None], (SGU_GROUPS, SGU_CHUNK, D_SGU // SGU_GROUPS))
        mb = _sgu(xs, i, pre_g, w_in, row(sgu_ln_g[i]), row(sgu_ln_b[i]), sgu_w_s, b_s, p_b)

        xs = _merge(xs, i, pre_g, w_in, a, p_a, mb, w_out, row(mix_post_g[i]))
        xs = _ffn(xs, i, row(ffn2_pre_g[i]), *ffn2, row(ffn2_post_g[i]))
    return xs.reshape(bsz, seq, d)
```

```python
import functools
import math

import jax
import jax.numpy as jnp
from jax import lax
from jax.experimental import pallas as pl
from jax.experimental.pallas import tpu as pltpu

F32 = jnp.float32
BF16 = jnp.bfloat16

D_MODEL = 2048
D_FF = 5632
D_HYENA = D_MODEL // 2
HYENA_ORDER = 2
SHORT_CONV = 3
FILTER_BANDS = 16
FILTER_EMB = 1 + 2 * FILTER_BANDS
FILTER_HIDDEN = 64
FILTER_INNER = 2
DECAY_FAST_PCT = 0.3
DECAY_SLOW_PCT = 1.5
DECAY_TARGET = 1e-2
N_DIRS = 2
D_SGU = D_MODEL // 2
SGU_CHUNK = 128
SGU_GROUPS = 8
NORM_EPS = 1e-6
LN_EPS = 1e-5

LANES = 128
SUBLANES = 8
VMEM_LIMIT_BYTES = 56 * 1024 * 1024
FFN_VMEM_LIMIT_BYTES = 61 * 1024 * 1024

FFN_TM = 1024
FFN_TF = 256
PROJ_TM = 512
MIX_TM = 512
FFT_N2 = 128
FFT_ROWS = 16
FFT_GROUPS = 2
SPECTRAL_K1 = 2
ROW_CHUNK = 16
ROW_UNROLL = True
W_IN_BLOCK = 1024


def _params(*sem, vmem=VMEM_LIMIT_BYTES):
    return pltpu.CompilerParams(dimension_semantics=sem, vmem_limit_bytes=vmem)


def _resident(shape, index_map):
    return pl.BlockSpec(shape, index_map, pipeline_mode=pl.Buffered(1))


def _rms(x):
    return x * lax.rsqrt(jnp.mean(x * x, axis=-1, keepdims=True) + NORM_EPS)


def _dot(a, b):
    return jnp.dot(a, b, preferred_element_type=F32)


def _row_chunks(n_rows, body, unroll=ROW_UNROLL):
    def step(r, carry):
        body(pl.ds(pl.multiple_of(r * ROW_CHUNK, ROW_CHUNK), ROW_CHUNK))
        return carry
    lax.fori_loop(0, n_rows // ROW_CHUNK, step, 0, unroll=unroll)


def _rms_scales(src_ref, scale_ref):
    def body(rows):
        a = src_ref[rows, :]
        scale_ref[rows, :] = lax.rsqrt(jnp.mean(a * a, axis=-1, keepdims=True) + NORM_EPS)
    _row_chunks(src_ref.shape[0], body)


def _ffn_kernel(x_ref, pre_g_ref, wg_ref, wu_ref, wd_ref, post_g_ref, o_ref, xn_ref, acc_ref, scale_ref):
    j = pl.program_id(1)
    tm = x_ref.shape[0]

    @pl.when(j == 0)
    def _():
        _rms_scales(x_ref, scale_ref)
        gain = pre_g_ref[...]

        def body(rows):
            xn_ref[rows, :] = (x_ref[rows, :] * scale_ref[rows, :] * gain).astype(BF16)
        _row_chunks(tm, body)

    def step(first):
        xn = xn_ref[...]
        g = _dot(xn, wg_ref[...].astype(BF16))
        u = _dot(xn, wu_ref[...].astype(BF16))
        h = (g * jax.nn.sigmoid(g) * u).astype(BF16)
        d = _dot(h, wd_ref[...].astype(BF16))
        acc_ref[...] = d if first else acc_ref[...] + d

    pl.when(j == 0)(functools.partial(step, True))
    pl.when(j > 0)(functools.partial(step, False))

    @pl.when(j == pl.num_programs(1) - 1)
    def _():
        _rms_scales(acc_ref, scale_ref)
        half_gain = 0.5 * post_g_ref[...]

        def body(rows):
            o_ref[rows, :] = x_ref[rows, :] + acc_ref[rows, :] * scale_ref[rows, :] * half_gain
        _row_chunks(tm, body)


def _ffn(x, layer, pre_g, wg, wu, wd, post_g):
    n, d = x.shape
    f = wg.shape[2]
    tm, tf = FFN_TM, FFN_TF
    return pl.pallas_call(
        _ffn_kernel,
        grid=(n // tm, f // tf),
        in_specs=[
            pl.BlockSpec((tm, d), lambda i, j: (i, 0)),
            pl.BlockSpec((1, d), lambda i, j: (0, 0)),
            pl.BlockSpec((None, d, tf), lambda i, j: (layer, 0, j)),
            pl.BlockSpec((None, d, tf), lambda i, j: (layer, 0, j)),
            pl.BlockSpec((None, tf, d), lambda i, j: (layer, j, 0)),
            pl.BlockSpec((1, d), lambda i, j: (0, 0)),
        ],
        out_specs=pl.BlockSpec((tm, d), lambda i, j: (i, 0)),
        out_shape=jax.ShapeDtypeStruct((n, d), F32),
        scratch_shapes=[pltpu.VMEM((tm, d), BF16), pltpu.VMEM((tm, d), F32), pltpu.VMEM((tm, 1), F32)],
        compiler_params=_params("parallel", "arbitrary", vmem=FFN_VMEM_LIMIT_BYTES),
        name="ffn",
    )(x, pre_g, wg, wu, wd, post_g)


def _proj_conv_kernel(xp_ref, x_ref, xn_ref, g_ref, w_ref, cw_ref, cb_ref, o_ref, *, tiles_per_seq):
    tm = x_ref.shape[0]
    pos = lax.rem(pl.program_id(0), tiles_per_seq)
    keep_prev = (pos != 0).astype(F32)
    keep_next = (pos != tiles_per_seq - 1).astype(F32)
    xa = jnp.concatenate([xp_ref[...] * keep_prev, x_ref[...], xn_ref[...] * keep_next], axis=0)
    xn = (_rms(xa) * g_ref[...]).astype(BF16)
    rows = xn.shape[0]
    mid = slice(SUBLANES, SUBLANES + tm)
    for part in range(o_ref.shape[0]):
        cols = slice(part * D_HYENA, (part + 1) * D_HYENA)
        p = _dot(xn, w_ref[:, cols])
        prev = pltpu.roll(p, 1, 0)[mid]
        nxt = pltpu.roll(p, rows - 1, 0)[mid]
        cw = cw_ref[:, cols]
        y = cb_ref[:, cols] + prev * cw[0:1] + p[mid] * cw[1:2] + nxt * cw[2:3]
        o_ref[part] = y.astype(o_ref.dtype)


def _proj_conv(x, layer, g, w_in, conv_w, conv_b, seq):
    n, d = x.shape
    parts = HYENA_ORDER + 1
    c = parts * D_HYENA
    tm = PROJ_TM
    halo_per_tile = tm // SUBLANES
    last_halo = n // SUBLANES - 1
    const2 = lambda i: (0, 0)
    return pl.pallas_call(
        functools.partial(_proj_conv_kernel, tiles_per_seq=seq // tm),
        grid=(n // tm,),
        in_specs=[
            pl.BlockSpec((SUBLANES, d), lambda i: (jnp.maximum(i * halo_per_tile - 1, 0), 0)),
            pl.BlockSpec((tm, d), lambda i: (i, 0)),
            pl.BlockSpec((SUBLANES, d), lambda i: (jnp.minimum((i + 1) * halo_per_tile, last_halo), 0)),
            pl.BlockSpec((1, d), const2),
            _resident((None, d, c), lambda i: (layer, 0, 0)),
            pl.BlockSpec((SHORT_CONV, c), const2),
            pl.BlockSpec((1, c), const2),
        ],
        out_specs=pl.BlockSpec((parts, tm, D_HYENA), lambda i: (0, i, 0)),
        out_shape=jax.ShapeDtypeStruct((parts, n, D_HYENA), BF16),
        compiler_params=_params("parallel"),
        name="hyena_proj_conv",
    )(x, x, x, g, w_in, conv_w, conv_b)


def _filter_kernel(z_ref, w1_ref, b_ref, freq_ref, wi_ref, wo_ref, delta_ref, g_ref, o_ref):
    hp = functools.partial(jnp.dot, preferred_element_type=F32, precision=lax.Precision.HIGHEST)
    h1, r, feat = z_ref.shape
    z = z_ref[...].reshape(h1 * r, feat)
    b = b_ref[...]
    freq = freq_ref[...]
    h = jnp.sin(freq[0:1] * (hp(z, w1_ref[...]) + b[0:1]))
    for j in range(FILTER_INNER):
        h = jnp.sin(freq[j + 1:j + 2] * (hp(h, wi_ref[j]) + b[j + 1:j + 2]))
    h = _dot(h.astype(BF16), wo_ref[...].astype(BF16))
    t = z[:, 0:1]
    decay = jnp.exp(-t * delta_ref[...])
    flat = lax.broadcasted_iota(jnp.int32, (h1 * r, D_HYENA), 0)
    lag0 = jnp.logical_and(flat == 0, pl.program_id(0) == 0)
    n1 = o_ref.shape[1]
    for blk in range(HYENA_ORDER * N_DIRS):
        hb = h[:, blk * D_HYENA:(blk + 1) * D_HYENA] * decay
        if blk % N_DIRS == 1:
            hb = jnp.where(lag0, 0.0, hb)
        a = _dot(g_ref[...], hb.astype(BF16))
        o_ref[blk] = a.astype(o_ref.dtype).reshape(n1, r, D_HYENA)


def _filter_stage1(z, w1, b, freq, w_inner, w_out, abs_delta, g1):
    h1, n2, feat = z.shape
    nsig = HYENA_ORDER * N_DIRS
    r = FFT_ROWS
    n1 = g1.shape[0] // r
    full = lambda a: pl.BlockSpec(a.shape, lambda i: (0,) * a.ndim)
    return pl.pallas_call(
        _filter_kernel,
        grid=(n2 // r,),
        in_specs=[pl.BlockSpec((h1, r, feat), lambda i: (0, i, 0)),
                  full(w1), full(b), full(freq), full(w_inner), full(w_out), full(abs_delta), full(g1)],
        out_specs=pl.BlockSpec((nsig, n1, r, D_HYENA), lambda i: (0, 0, i, 0)),
        out_shape=jax.ShapeDtypeStruct((nsig, n1, n2, D_HYENA), BF16),
        compiler_params=_params("parallel"),
        name="filter_taps",
    )(z, w1, b, freq, w_inner, w_out, abs_delta, g1)


def _slab_groups(rows_per_block):
    return [slice(g * FFT_ROWS, (g + 1) * FFT_ROWS) for g in range(rows_per_block // FFT_ROWS)]


def _stage1_kernel(g_ref, x_ref, o_ref):
    h1, rb, c = x_ref.shape[1:]
    n1 = o_ref.shape[1]
    for rows in _slab_groups(rb):
        x = x_ref[0, :, rows, :].reshape(h1 * FFT_ROWS, c)
        o_ref[0, :, rows, :] = _dot(g_ref[...], x).astype(o_ref.dtype).reshape(n1, FFT_ROWS, c)


def _stage1(g1, x, part):
    _, nsig, h1, n2, c = x.shape
    r = FFT_ROWS * FFT_GROUPS
    n1 = g1.shape[0] // FFT_ROWS
    return pl.pallas_call(
        _stage1_kernel,
        grid=(nsig, n2 // r),
        in_specs=[
            pl.BlockSpec(g1.shape, lambda s, j: (0, 0)),
            pl.BlockSpec((None, 1, h1, r, c), lambda s, j: (part, s, 0, j, 0)),
        ],
        out_specs=pl.BlockSpec((1, n1, r, c), lambda s, j: (s, 0, j, 0)),
        out_shape=jax.ShapeDtypeStruct((nsig, n1, n2, c), BF16),
        compiler_params=_params("parallel", "parallel"),
        name="fft_stage1",
    )(g1, x)


def _filter_spectrum_kernel(t_ref, a_ref, k_ref, *, scale):
    t = t_ref[0]
    half = t.shape[0] // 2
    for o in range(HYENA_ORDER):
        xf = _dot(t, a_ref[N_DIRS * o, 0])
        xb = _dot(t, a_ref[N_DIRS * o + 1, 0])
        k_ref[o, 0, :half, :] = ((xf[:half] + xb[:half]) * scale).astype(k_ref.dtype)
        k_ref[o, 0, half:, :] = ((xf[half:] - xb[half:]) * scale).astype(k_ref.dtype)


def _filter_spectrum(t2, a, seq):
    nsig, h1, rows, c = a.shape
    return pl.pallas_call(
        functools.partial(_filter_spectrum_kernel, scale=1.0 / seq),
        grid=(h1,),
        in_specs=[
            pl.BlockSpec((1, rows, rows), lambda k: (k, 0, 0)),
            pl.BlockSpec((nsig, 1, rows, c), lambda k: (0, k, 0, 0)),
        ],
        out_specs=pl.BlockSpec((HYENA_ORDER, 1, rows, c), lambda k: (0, k, 0, 0)),
        out_shape=jax.ShapeDtypeStruct((HYENA_ORDER, h1, rows, c), BF16),
        compiler_params=_params("parallel"),
        name="filter_spectrum",
    )(t2, a)


def _spectral_kernel(t_ref, tt_ref, a_ref, k_ref, o_ref):
    for q in range(t_ref.shape[0]):
        t = t_ref[q]
        tt = tt_ref[q]
        k = k_ref[0, q].astype(F32)
        half = k.shape[0] // 2
        kr, ki = k[:half], k[half:]
        for b in range(a_ref.shape[0]):
            x = _dot(t, a_ref[b, q])
            xr, xi = x[:half], x[half:]
            z = jnp.concatenate([xr * kr - xi * ki, xr * ki + xi * kr], axis=0).astype(BF16)
            o_ref[b, q] = _dot(tt, z).astype(o_ref.dtype)


def _spectral(t2, t2t, a, k, order):
    bsz, h1, rows, c = a.shape
    kb = SPECTRAL_K1
    return pl.pallas_call(
        _spectral_kernel,
        grid=(h1 // kb,),
        in_specs=[
            pl.BlockSpec((kb, rows, rows), lambda q: (q, 0, 0)),
            pl.BlockSpec((kb, rows, rows), lambda q: (q, 0, 0)),
            pl.BlockSpec((bsz, kb, rows, c), lambda q: (0, q, 0, 0)),
            pl.BlockSpec((1, kb, rows, c), lambda q: (order, q, 0, 0)),
        ],
        out_specs=pl.BlockSpec((bsz, kb, rows, c), lambda q: (0, q, 0, 0)),
        out_shape=jax.ShapeDtypeStruct((bsz, h1, rows, c), BF16),
        compiler_params=_params("parallel"),
        name="hyena_spectral",
    )(t2, t2t, a, k)


def _inverse_gate_kernel(gt_ref, b_ref, gate_ref, y_ref, skip_ref, *rest, emit_stage1):
    if emit_stage1:
        g_ref, o_ref, a_ref = rest
    else:
        (o_ref,), a_ref = rest, None
    n1, rb, c = b_ref.shape[1:]
    h1 = o_ref.shape[2]
    for rows in _slab_groups(rb):
        conv = _dot(gt_ref[...], b_ref[0, :, rows, :].reshape(n1 * FFT_ROWS, c))
        y = y_ref[0, :, rows, :].reshape(conv.shape).astype(F32)
        gate = gate_ref[0, :, rows, :].reshape(conv.shape).astype(F32)
        out = (gate * (conv + y * skip_ref[...])).astype(BF16)
        o_ref[0, 0, :, rows, :] = out.reshape(h1, FFT_ROWS, c)
        if a_ref is not None:
            a_ref[0, :, rows, :] = _dot(g_ref[...], out).astype(BF16).reshape(n1, FFT_ROWS, c)


def _inverse_gate(g1t, b, gate, gate_part, y, y_part, skip, g1=None):
    bsz, n1, n2, c = b.shape
    r = FFT_ROWS * FFT_GROUPS
    h1 = g1t.shape[0] // FFT_ROWS
    const2 = lambda s, j: (0, 0)
    in_specs = [
        pl.BlockSpec(g1t.shape, const2),
        pl.BlockSpec((1, n1, r, c), lambda s, j: (s, 0, j, 0)),
        pl.BlockSpec((None, 1, h1, r, c), lambda s, j: (gate_part, s, 0, j, 0)),
        pl.BlockSpec((None, 1, h1, r, c), lambda s, j: (y_part, s, 0, j, 0)),
        pl.BlockSpec((1, c), const2),
    ]
    out_specs = [pl.BlockSpec((1, 1, h1, r, c), lambda s, j: (0, s, 0, j, 0))]
    out_shape = [jax.ShapeDtypeStruct((1, bsz, h1, n2, c), BF16)]
    args = [g1t, b, gate, y, skip]
    if g1 is not None:
        in_specs.append(pl.BlockSpec(g1.shape, const2))
        out_specs.append(pl.BlockSpec((1, n1, r, c), lambda s, j: (s, 0, j, 0)))
        out_shape.append(jax.ShapeDtypeStruct((bsz, n1, n2, c), BF16))
        args.append(g1)
    return pl.pallas_call(
        functools.partial(_inverse_gate_kernel, emit_stage1=g1 is not None),
        grid=(bsz, n2 // r),
        in_specs=in_specs,
        out_specs=out_specs,
        out_shape=out_shape,
        compiler_params=_params("parallel", "parallel"),
        name="hyena_inverse_gate",
    )(*args)


def _sgu_kernel(x_ref, g_ref, wu_ref, wv_ref, wg0_ref, wg1_ref, lng_ref, lnb_ref, ws_ref, bs_ref,
                pb_ref, o_ref, s_ref):
    xn = (_rms(x_ref[...]) * g_ref[...]).astype(BF16)
    u = jax.nn.gelu(_dot(xn, wu_ref[...]), approximate=True)
    v = jax.nn.gelu(_dot(xn, wv_ref[...]), approximate=True)
    mu = jnp.mean(v, axis=-1, keepdims=True)
    vc = v - mu
    var = jnp.mean(vc * vc, axis=-1, keepdims=True)
    v = (vc * lax.rsqrt(var + LN_EPS) * lng_ref[...] + lnb_ref[...]).astype(BF16)
    dg = D_SGU // SGU_GROUPS
    for c in range(x_ref.shape[0] // SGU_CHUNK):
        rows = slice(c * SGU_CHUNK, (c + 1) * SGU_CHUNK)
        for grp in range(SGU_GROUPS):
            cols = slice(grp * dg, (grp + 1) * dg)
            s_ref[rows, cols] = _dot(ws_ref[grp], v[rows, cols]) + bs_ref[grp]
    gated = (u * s_ref[...]).astype(BF16)
    branch = _dot(gated, pb_ref[...])
    for h, wg_ref in enumerate((wg0_ref, wg1_ref)):
        cols = slice(h * W_IN_BLOCK, (h + 1) * W_IN_BLOCK)
        gate = jax.nn.sigmoid(_dot(xn, wg_ref[...]))
        o_ref[:, cols] = (gate * branch[:, cols]).astype(o_ref.dtype)


def _sgu(x, layer, g, w_in, ln_g, ln_b, w_s, b_s, p_b):
    n, d = x.shape
    tm = MIX_TM
    const2 = lambda i: (0, 0)
    first = (HYENA_ORDER + 1) * D_HYENA // W_IN_BLOCK
    gate_b = first + 2 * D_SGU // W_IN_BLOCK + d // W_IN_BLOCK
    w_blk = lambda blk: _resident((None, d, W_IN_BLOCK), lambda i: (layer, 0, blk))
    return pl.pallas_call(
        _sgu_kernel,
        grid=(n // tm,),
        in_specs=[
            pl.BlockSpec((tm, d), lambda i: (i, 0)),
            pl.BlockSpec((1, d), const2),
            w_blk(first), w_blk(first + 1), w_blk(gate_b), w_blk(gate_b + 1),
            pl.BlockSpec(ln_g.shape, const2),
            pl.BlockSpec(ln_b.shape, const2),
            _resident((None,) + w_s.shape[1:], lambda i: (layer, 0, 0, 0)),
            _resident(b_s.shape, lambda i: (0, 0, 0)),
            _resident((None,) + p_b.shape[1:], lambda i: (layer, 0, 0)),
        ],
        out_specs=pl.BlockSpec((tm, d), lambda i: (i, 0)),
        out_shape=jax.ShapeDtypeStruct((n, d), BF16),
        scratch_shapes=[pltpu.VMEM((tm, D_SGU), F32)],
        compiler_params=_params("parallel"),
        name="sgu_branch",
    )(x, g, w_in, w_in, w_in, w_in, ln_g, ln_b, w_s, b_s, p_b)


def _merge_kernel(x_ref, g_ref, wg0_ref, wg1_ref, a_ref, pa_ref, mb_ref, wout_ref, post_g_ref, o_ref):
    x = x_ref[...]
    xn = (_rms(x) * g_ref[...]).astype(BF16)
    pa = _dot(a_ref[...], pa_ref[...])
    halves = []
    for h, wg_ref in enumerate((wg0_ref, wg1_ref)):
        cols = slice(h * W_IN_BLOCK, (h + 1) * W_IN_BLOCK)
        gate = jax.nn.sigmoid(_dot(xn, wg_ref[...]))
        halves.append((gate * pa[:, cols] + mb_ref[:, cols].astype(F32)).astype(BF16))
    y = _dot(jnp.concatenate(halves, axis=1), wout_ref[...])
    o_ref[...] = x + _rms(y) * post_g_ref[...]


def _merge(x, layer, g, w_in, a, p_a, mb, w_out, post_g):
    n, d = x.shape
    tm = MIX_TM
    const2 = lambda i: (0, 0)
    gate_a = ((HYENA_ORDER + 1) * D_HYENA + 2 * D_SGU) // W_IN_BLOCK
    w_blk = lambda blk: _resident((None, d, W_IN_BLOCK), lambda i: (layer, 0, blk))
    return pl.pallas_call(
        _merge_kernel,
        grid=(n // tm,),
        in_specs=[
            pl.BlockSpec((tm, d), lambda i: (i, 0)),
            pl.BlockSpec((1, d), const2),
            w_blk(gate_a), w_blk(gate_a + 1),
            pl.BlockSpec((tm, a.shape[1]), lambda i: (i, 0)),
            _resident((None,) + p_a.shape[1:], lambda i: (layer, 0, 0)),
            pl.BlockSpec((tm, d), lambda i: (i, 0)),
            _resident((None,) + w_out.shape[1:], lambda i: (layer, 0, 0)),
            pl.BlockSpec((1, d), const2),
        ],
        out_specs=pl.BlockSpec((tm, d), lambda i: (i, 0)),
        out_shape=jax.ShapeDtypeStruct((n, d), F32),
        compiler_params=_params("parallel"),
        name="merge",
    )(x, g, w_in, w_in, a, p_a, mb, w_out, post_g)


def _filter_positions(seq):
    t = jnp.linspace(0.0, 1.0, seq, dtype=F32)[:, None]
    w = (2.0 * math.pi / seq) * jnp.arange(seq, dtype=F32)[:, None]
    f = jnp.linspace(1e-4, FILTER_BANDS - 1, FILTER_BANDS, dtype=F32)[None, :]
    z = jnp.concatenate([t, jnp.cos(f * w), -jnp.sin(f * w)], axis=-1)
    return jnp.pad(z, ((0, 0), (0, LANES - FILTER_EMB)))


def _decay_rates():
    max_decay = math.log(DECAY_TARGET) / DECAY_FAST_PCT
    min_decay = math.log(DECAY_TARGET) / DECAY_SLOW_PCT
    return jnp.abs(jnp.linspace(min_decay, max_decay, D_HYENA, dtype=F32))[None, :]


def _fft_tables(seq):
    n = 2 * seq
    n2 = FFT_N2
    n1 = n // n2
    h1 = n1 // 2
    k1 = jnp.arange(h1, dtype=jnp.int32)
    m1 = ((2 * k1[:, None] + 1) * k1[None, :]) % (2 * n1)
    a1 = m1.astype(F32) * (math.pi / n1)
    f1 = jnp.stack([jnp.cos(a1), -jnp.sin(a1)], axis=1).reshape(n1, h1)
    q = jnp.arange(n2, dtype=jnp.int32)
    kk = 2 * (k1[:, None, None] + n1 * q[None, :, None]) + 1
    a2 = ((kk * q[None, None, :]) % (2 * n)).astype(F32) * (math.pi / n)
    c, s = jnp.cos(a2), jnp.sin(a2)
    t2 = jnp.concatenate([jnp.concatenate([c, s], axis=2), jnp.concatenate([-s, c], axis=2)], axis=1)
    eye = jnp.eye(FFT_ROWS, dtype=F32)
    g1, g1t = jnp.kron(f1, eye), jnp.kron(f1.T, eye)
    return (g1.astype(BF16), g1t.astype(BF16), t2.astype(BF16), jnp.swapaxes(t2, 1, 2).astype(BF16))


def kernel(x, ffn1_pre_g, ffn1_w_gate, ffn1_w_up, ffn1_w_down, ffn1_post_g, mix_pre_g, w_in, hy_conv_w, hy_conv_b, filt_w1, filt_b, filt_freq, filt_w_inner, filt_w_out, hy_skip, sgu_ln_g, sgu_ln_b, sgu_w_s, sgu_b_s, p_a, p_b, w_out, mix_post_g, ffn2_pre_g, ffn2_w_gate, ffn2_w_up, ffn2_w_down, ffn2_post_g):
    bsz, seq, d = x.shape
    depth = w_in.shape[0]
    n_tok = bsz * seq

    z = _filter_positions(seq)
    abs_delta = _decay_rates()
    g1, g1t, t2, t2t = _fft_tables(seq)
    n2 = FFT_N2
    h1 = seq // n2
    n1 = 2 * h1
    nsig = HYENA_ORDER * N_DIRS
    row = lambda v: v[None, :]
    bf = lambda w: w.astype(BF16)

    ffn1 = (ffn1_w_gate, ffn1_w_up, ffn1_w_down)
    ffn2 = (ffn2_w_gate, ffn2_w_up, ffn2_w_down)
    w_in, p_a, p_b, w_out, sgu_w_s = bf(w_in), bf(p_a), bf(p_b), bf(w_out), bf(sgu_w_s)

    xs = x.reshape(n_tok, d)
    for i in range(depth):
        xs = _ffn(xs, i, row(ffn1_pre_g[i]), *ffn1, row(ffn1_post_g[i]))
        pre_g = row(mix_pre_g[i])

        u = _proj_conv(xs, i, pre_g, w_in, hy_conv_w[i], row(hy_conv_b[i]), seq)
        u = u.reshape(HYENA_ORDER + 1, bsz, h1, n2, D_HYENA)
        w1 = jnp.pad(filt_w1[i], ((0, LANES - FILTER_EMB), (0, 0)))
        taps_a = _filter_stage1(z.reshape(h1, n2, LANES), w1, filt_b[i], filt_freq[i], filt_w_inner[i],
                                filt_w_out[i], abs_delta, g1)
        k_spec = _filter_spectrum(t2, taps_a.reshape(nsig, h1, 2 * n2, D_HYENA), seq)
        y, y_part = u, 0
        a1 = _stage1(g1, y, y_part)
        for o in range(HYENA_ORDER):
            last = o == HYENA_ORDER - 1
            b2 = _spectral(t2, t2t, a1.reshape(bsz, h1, 2 * n2, D_HYENA), k_spec, o)
            outs = _inverse_gate(g1t, b2.reshape(bsz, n1, n2, D_HYENA), u, o + 1, y, y_part,
                                 row(hy_skip[i, o]), g1=None if last else g1)
            y, y_part = outs[0], 0
            a1 = None if last else outs[1]
        a = y.reshape(n_tok, D_HYENA)

        b_s = jnp.broadcast_to(sgu_b_s[i][:, :, None], (SGU_GROUPS, SGU_CHUNK, D_SGU // SGU_GROUPS))
        mb = _sgu(xs, i, pre_g, w_in, row(sgu_ln_g[i]), row(sgu_ln_b[i]), sgu_w_s, b_s, p_b)

        xs = _merge(xs, i, pre_g, w_in, a, p_a, mb, w_out, row(mix_post_g[i]))
        xs = _ffn(xs, i, row(ffn2_pre_g[i]), *ffn2, row(ffn2_post_g[i]))
    return xs.reshape(bsz, seq, d)
```

```python
import functools
import math

import jax
import jax.numpy as jnp
from jax import lax
from jax.experimental import pallas as pl
from jax.experimental.pallas import tpu as pltpu

F32 = jnp.float32
BF16 = jnp.bfloat16

D_MODEL = 2048
D_FF = 5632
D_HYENA = D_MODEL // 2
HYENA_ORDER = 2
SHORT_CONV = 3
FILTER_BANDS = 16
FILTER_EMB = 1 + 2 * FILTER_BANDS
FILTER_HIDDEN = 64
FILTER_INNER = 2
DECAY_FAST_PCT = 0.3
DECAY_SLOW_PCT = 1.5
DECAY_TARGET = 1e-2
N_DIRS = 2
D_SGU = D_MODEL // 2
SGU_CHUNK = 128
SGU_GROUPS = 8
NORM_EPS = 1e-6
LN_EPS = 1e-5

LANES = 128
SUBLANES = 8
VMEM_LIMIT_BYTES = 56 * 1024 * 1024
FFN_VMEM_LIMIT_BYTES = 61 * 1024 * 1024

FFN_TM = 1024
FFN_TF = 256
PROJ_TM = 512
MIX_TM = 512
FFT_N2 = 128
FFT_ROWS = 16
FFT_GROUPS = 2
SPECTRAL_K1 = 2
ROW_CHUNK = 16
ROW_UNROLL = True
W_IN_BLOCK = 1024


def _params(*sem, vmem=VMEM_LIMIT_BYTES):
    return pltpu.CompilerParams(dimension_semantics=sem, vmem_limit_bytes=vmem)


def _resident(shape, index_map):
    return pl.BlockSpec(shape, index_map, pipeline_mode=pl.Buffered(1))


def _rms(x):
    return x * lax.rsqrt(jnp.mean(x * x, axis=-1, keepdims=True) + NORM_EPS)


def _dot(a, b):
    return jnp.dot(a, b, preferred_element_type=F32)


def _row_chunks(n_rows, body, unroll=ROW_UNROLL):
    def step(r, carry):
        body(pl.ds(pl.multiple_of(r * ROW_CHUNK, ROW_CHUNK), ROW_CHUNK))
        return carry
    lax.fori_loop(0, n_rows // ROW_CHUNK, step, 0, unroll=unroll)


def _rms_scales(src_ref, scale_ref):
    def body(rows):
        a = src_ref[rows, :]
        scale_ref[rows, :] = lax.rsqrt(jnp.mean(a * a, axis=-1, keepdims=True) + NORM_EPS)
    _row_chunks(src_ref.shape[0], body)


def _ffn_kernel(x_ref, pre_g_ref, wg_ref, wu_ref, wd_ref, post_g_ref, o_ref, xn_ref, acc_ref, scale_ref):
    j = pl.program_id(1)
    tm = x_ref.shape[0]

    @pl.when(j == 0)
    def _():
        _rms_scales(x_ref, scale_ref)
        gain = pre_g_ref[...]

        def body(rows):
            xn_ref[rows, :] = (x_ref[rows, :] * scale_ref[rows, :] * gain).astype(BF16)
        _row_chunks(tm, body)

    def step(first):
        xn = xn_ref[...]
        g = _dot(xn, wg_ref[...].astype(BF16))
        u = _dot(xn, wu_ref[...].astype(BF16))
        h = (g * jax.nn.sigmoid(g) * u).astype(BF16)
        d = _dot(h, wd_ref[...].astype(BF16))
        acc_ref[...] = d if first else acc_ref[...] + d

    pl.when(j == 0)(functools.partial(step, True))
    pl.when(j > 0)(functools.partial(step, False))

    @pl.when(j == pl.num_programs(1) - 1)
    def _():
        _rms_scales(acc_ref, scale_ref)
        half_gain = 0.5 * post_g_ref[...]

        def body(rows):
            o_ref[rows, :] = x_ref[rows, :] + acc_ref[rows, :] * scale_ref[rows, :] * half_gain
        _row_chunks(tm, body)


def _ffn(x, layer, pre_g, wg, wu, wd, post_g):
    n, d = x.shape
    f = wg.shape[2]
    tm, tf = FFN_TM, FFN_TF
    return pl.pallas_call(
        _ffn_kernel,
        grid=(n // tm, f // tf),
        in_specs=[
            pl.BlockSpec((tm, d), lambda i, j: (i, 0)),
            pl.BlockSpec((1, d), lambda i, j: (0, 0)),
            pl.BlockSpec((None, d, tf), lambda i, j: (layer, 0, j)),
            pl.BlockSpec((None, d, tf), lambda i, j: (layer, 0, j)),
            pl.BlockSpec((None, tf, d), lambda i, j: (layer, j, 0)),
            pl.BlockSpec((1, d), lambda i, j: (0, 0)),
        ],
        out_specs=pl.BlockSpec((tm, d), lambda i, j: (i, 0)),
        out_shape=jax.ShapeDtypeStruct((n, d), F32),
        scratch_shapes=[pltpu.VMEM((tm, d), BF16), pltpu.VMEM((tm, d), F32), pltpu.VMEM((tm, 1), F32)],
        compiler_params=_params("parallel", "arbitrary", vmem=FFN_VMEM_LIMIT_BYTES),
        name="ffn",
    )(x, pre_g, wg, wu, wd, post_g)


def _proj_conv_kernel(xp_ref, x_ref, xn_ref, g_ref, w_ref, cw_ref, cb_ref, o_ref, *, tiles_per_seq):
    tm = x_ref.shape[0]
    pos = lax.rem(pl.program_id(0), tiles_per_seq)
    keep_prev = (pos != 0).astype(F32)
    keep_next = (pos != tiles_per_seq - 1).astype(F32)
    xa = jnp.concatenate([xp_ref[...] * keep_prev, x_ref[...], xn_ref[...] * keep_next], axis=0)
    xn = (_rms(xa) * g_ref[...]).astype(BF16)
    rows = xn.shape[0]
    mid = slice(SUBLANES, SUBLANES + tm)
    for part in range(o_ref.shape[0]):
        cols = slice(part * D_HYENA, (part + 1) * D_HYENA)
        p = _dot(xn, w_ref[:, cols])
        prev = pltpu.roll(p, 1, 0)[mid]
        nxt = pltpu.roll(p, rows - 1, 0)[mid]
        cw = cw_ref[:, cols]
        y = cb_ref[:, cols] + prev * cw[0:1] + p[mid] * cw[1:2] + nxt * cw[2:3]
        o_ref[part] = y.astype(o_ref.dtype)


def _proj_conv(x, layer, g, w_in, conv_w, conv_b, seq):
    n, d = x.shape
    parts = HYENA_ORDER + 1
    c = parts * D_HYENA
    tm = PROJ_TM
    halo_per_tile = tm // SUBLANES
    last_halo = n // SUBLANES - 1
    const2 = lambda i: (0, 0)
    return pl.pallas_call(
        functools.partial(_proj_conv_kernel, tiles_per_seq=seq // tm),
        grid=(n // tm,),
        in_specs=[
            pl.BlockSpec((SUBLANES, d), lambda i: (jnp.maximum(i * halo_per_tile - 1, 0), 0)),
            pl.BlockSpec((tm, d), lambda i: (i, 0)),
            pl.BlockSpec((SUBLANES, d), lambda i: (jnp.minimum((i + 1) * halo_per_tile, last_halo), 0)),
            pl.BlockSpec((1, d), const2),
            _resident((None, d, c), lambda i: (layer, 0, 0)),
            pl.BlockSpec((SHORT_CONV, c), const2),
            pl.BlockSpec((1, c), const2),
        ],
        out_specs=pl.BlockSpec((parts, tm, D_HYENA), lambda i: (0, i, 0)),
        out_shape=jax.ShapeDtypeStruct((parts, n, D_HYENA), BF16),
        compiler_params=_params("parallel"),
        name="hyena_proj_conv",
    )(x, x, x, g, w_in, conv_w, conv_b)


def _filter_kernel(z_ref, w1_ref, b_ref, freq_ref, wi_ref, wo_ref, delta_ref, g_ref, o_ref):
    hp = functools.partial(jnp.dot, preferred_element_type=F32, precision=lax.Precision.HIGHEST)
    h1, r, feat = z_ref.shape
    z = z_ref[...].reshape(h1 * r, feat)
    b = b_ref[...]
    freq = freq_ref[...]
    h = jnp.sin(freq[0:1] * (hp(z, w1_ref[...]) + b[0:1]))
    for j in range(FILTER_INNER):
        h = jnp.sin(freq[j + 1:j + 2] * (hp(h, wi_ref[j]) + b[j + 1:j + 2]))
    h = _dot(h.astype(BF16), wo_ref[...].astype(BF16))
    t = z[:, 0:1]
    decay = jnp.exp(-t * delta_ref[...])
    flat = lax.broadcasted_iota(jnp.int32, (h1 * r, D_HYENA), 0)
    lag0 = jnp.logical_and(flat == 0, pl.program_id(0) == 0)
    n1 = o_ref.shape[1]
    for blk in range(HYENA_ORDER * N_DIRS):
        hb = h[:, blk * D_HYENA:(blk + 1) * D_HYENA] * decay
        if blk % N_DIRS == 1:
            hb = jnp.where(lag0, 0.0, hb)
        a = _dot(g_ref[...], hb.astype(BF16))
        o_ref[blk] = a.astype(o_ref.dtype).reshape(n1, r, D_HYENA)


def _filter_stage1(z, w1, b, freq, w_inner, w_out, abs_delta, g1):
    h1, n2, feat = z.shape
    nsig = HYENA_ORDER * N_DIRS
    r = FFT_ROWS
    n1 = g1.shape[0] // r
    full = lambda a: pl.BlockSpec(a.shape, lambda i: (0,) * a.ndim)
    return pl.pallas_call(
        _filter_kernel,
        grid=(n2 // r,),
        in_specs=[pl.BlockSpec((h1, r, feat), lambda i: (0, i, 0)),
                  full(w1), full(b), full(freq), full(w_inner), full(w_out), full(abs_delta), full(g1)],
        out_specs=pl.BlockSpec((nsig, n1, r, D_HYENA), lambda i: (0, 0, i, 0)),
        out_shape=jax.ShapeDtypeStruct((nsig, n1, n2, D_HYENA), BF16),
        compiler_params=_params("parallel"),
        name="filter_taps",
    )(z, w1, b, freq, w_inner, w_out, abs_delta, g1)


def _slab_groups(rows_per_block):
    return [slice(g * FFT_ROWS, (g + 1) * FFT_ROWS) for g in range(rows_per_block // FFT_ROWS)]


def _stage1_kernel(g_ref, x_ref, o_ref):
    h1, rb, c = x_ref.shape[1:]
    n1 = o_ref.shape[1]
    for rows in _slab_groups(rb):
        x = x_ref[0, :, rows, :].reshape(h1 * FFT_ROWS, c)
        o_ref[0, :, rows, :] = _dot(g_ref[...], x).astype(o_ref.dtype).reshape(n1, FFT_ROWS, c)


def _stage1(g1, x, part):
    _, nsig, h1, n2, c = x.shape
    r = FFT_ROWS * FFT_GROUPS
    n1 = g1.shape[0] // FFT_ROWS
    return pl.pallas_call(
        _stage1_kernel,
        grid=(nsig, n2 // r),
        in_specs=[
            pl.BlockSpec(g1.shape, lambda s, j: (0, 0)),
            pl.BlockSpec((None, 1, h1, r, c), lambda s, j: (part, s, 0, j, 0)),
        ],
        out_specs=pl.BlockSpec((1, n1, r, c), lambda s, j: (s, 0, j, 0)),
        out_shape=jax.ShapeDtypeStruct((nsig, n1, n2, c), BF16),
        compiler_params=_params("parallel", "parallel"),
        name="fft_stage1",
    )(g1, x)


def _filter_spectrum_kernel(t_ref, a_ref, k_ref, *, scale):
    t = t_ref[0]
    half = t.shape[0] // 2
    for o in range(HYENA_ORDER):
        xf = _dot(t, a_ref[N_DIRS * o, 0])
        xb = _dot(t, a_ref[N_DIRS * o + 1, 0])
        k_ref[o, 0, :half, :] = ((xf[:half] + xb[:half]) * scale).astype(k_ref.dtype)
        k_ref[o, 0, half:, :] = ((xf[half:] - xb[half:]) * scale).astype(k_ref.dtype)


def _filter_spectrum(t2, a, seq):
    nsig, h1, rows, c = a.shape
    return pl.pallas_call(
        functools.partial(_filter_spectrum_kernel, scale=1.0 / seq),
        grid=(h1,),
        in_specs=[
            pl.BlockSpec((1, rows, rows), lambda k: (k, 0, 0)),
            pl.BlockSpec((nsig, 1, rows, c), lambda k: (0, k, 0, 0)),
        ],
        out_specs=pl.BlockSpec((HYENA_ORDER, 1, rows, c), lambda k: (0, k, 0, 0)),
        out_shape=jax.ShapeDtypeStruct((HYENA_ORDER, h1, rows, c), BF16),
        compiler_params=_params("parallel"),
        name="filter_spectrum",
    )(t2, a)


def _spectral_kernel(t_ref, tt_ref, a_ref, k_ref, o_ref):
    for q in range(t_ref.shape[0]):
        t = t_ref[q]
        tt = tt_ref[q]
        k = k_ref[0, q].astype(F32)
        half = k.shape[0] // 2
        kr, ki = k[:half], k[half:]
        for b in range(a_ref.shape[0]):
            x = _dot(t, a_ref[b, q])
            xr, xi = x[:half], x[half:]
            z = jnp.concatenate([xr * kr - xi * ki, xr * ki + xi * kr], axis=0).astype(BF16)
            o_ref[b, q] = _dot(tt, z).astype(o_ref.dtype)


def _spectral(t2, t2t, a, k, order):
    bsz, h1, rows, c = a.shape
    kb = SPECTRAL_K1
    return pl.pallas_call(
        _spectral_kernel,
        grid=(h1 // kb,),
        in_specs=[
            pl.BlockSpec((kb, rows, rows), lambda q: (q, 0, 0)),
            pl.BlockSpec((kb, rows, rows), lambda q: (q, 0, 0)),
            pl.BlockSpec((bsz, kb, rows, c), lambda q: (0, q, 0, 0)),
            pl.BlockSpec((1, kb, rows, c), lambda q: (order, q, 0, 0)),
        ],
        out_specs=pl.BlockSpec((bsz, kb, rows, c), lambda q: (0, q, 0, 0)),
        out_shape=jax.ShapeDtypeStruct((bsz, h1, rows, c), BF16),
        compiler_params=_params("parallel"),
        name="hyena_spectral",
    )(t2, t2t, a, k)


def _inverse_gate_kernel(gt_ref, b_ref, gate_ref, y_ref, skip_ref, *rest, emit_stage1):
    if emit_stage1:
        g_ref, o_ref, a_ref = rest
    else:
        (o_ref,), a_ref = rest, None
    n1, rb, c = b_ref.shape[1:]
    h1 = o_ref.shape[2]
    for rows in _slab_groups(rb):
        conv = _dot(gt_ref[...], b_ref[0, :, rows, :].reshape(n1 * FFT_ROWS, c))
        y = y_ref[0, :, rows, :].reshape(conv.shape).astype(F32)
        gate = gate_ref[0, :, rows, :].reshape(conv.shape).astype(F32)
        out = (gate * (conv + y * skip_ref[...])).astype(BF16)
        o_ref[0, 0, :, rows, :] = out.reshape(h1, FFT_ROWS, c)
        if a_ref is not None:
            a_ref[0, :, rows, :] = _dot(g_ref[...], out).astype(BF16).reshape(n1, FFT_ROWS, c)


def _inverse_gate(g1t, b, gate, gate_part, y, y_part, skip, g1=None):
    bsz, n1, n2, c = b.shape
    r = FFT_ROWS * FFT_GROUPS
    h1 = g1t.shape[0] // FFT_ROWS
    const2 = lambda s, j: (0, 0)
    in_specs = [
        pl.BlockSpec(g1t.shape, const2),
        pl.BlockSpec((1, n1, r, c), lambda s, j: (s, 0, j, 0)),
        pl.BlockSpec((None, 1, h1, r, c), lambda s, j: (gate_part, s, 0, j, 0)),
        pl.BlockSpec((None, 1, h1, r, c), lambda s, j: (y_part, s, 0, j, 0)),
        pl.BlockSpec((1, c), const2),
    ]
    out_specs = [pl.BlockSpec((1, 1, h1, r, c), lambda s, j: (0, s, 0, j, 0))]
    out_shape = [jax.ShapeDtypeStruct((1, bsz, h1, n2, c), BF16)]
    args = [g1t, b, gate, y, skip]
    if g1 is not None:
        in_specs.append(pl.BlockSpec(g1.shape, const2))
        out_specs.append(pl.BlockSpec((1, n1, r, c), lambda s, j: (s, 0, j, 0)))
        out_shape.append(jax.ShapeDtypeStruct((bsz, n1, n2, c), BF16))
        args.append(g1)
    return pl.pallas_call(
        functools.partial(_inverse_gate_kernel, emit_stage1=g1 is not None),
        grid=(bsz, n2 // r),
        in_specs=in_specs,
        out_specs=out_specs,
        out_shape=out_shape,
        compiler_params=_params("parallel", "parallel"),
        name="hyena_inverse_gate",
    )(*args)


def _sgu_kernel(x_ref, g_ref, wu_ref, wv_ref, wg0_ref, wg1_ref, lng_ref, lnb_ref, ws_ref, bs_ref,
                pb_ref, o_ref, s_ref, gate_ref):
    xn = (_rms(x_ref[...]) * g_ref[...]).astype(BF16)
    v = jax.nn.gelu(_dot(xn, wv_ref[...]), approximate=True)
    mu = jnp.mean(v, axis=-1, keepdims=True)
    vc = v - mu
    var = jnp.mean(vc * vc, axis=-1, keepdims=True)
    v = (vc * lax.rsqrt(var + LN_EPS) * lng_ref[...] + lnb_ref[...]).astype(BF16)
    for h, wg_ref in enumerate((wg0_ref, wg1_ref)):
        gate_ref[:, h * W_IN_BLOCK:(h + 1) * W_IN_BLOCK] = jax.nn.sigmoid(_dot(xn, wg_ref[...]))
    u = jax.nn.gelu(_dot(xn, wu_ref[...]), approximate=True)
    dg = D_SGU // SGU_GROUPS
    for c in range(x_ref.shape[0] // SGU_CHUNK):
        rows = slice(c * SGU_CHUNK, (c + 1) * SGU_CHUNK)
        for grp in range(SGU_GROUPS):
            cols = slice(grp * dg, (grp + 1) * dg)
            s_ref[rows, cols] = _dot(ws_ref[grp], v[rows, cols]) + bs_ref[grp]
    gated = (u * s_ref[...]).astype(BF16)
    branch = _dot(gated, pb_ref[...])
    o_ref[...] = (gate_ref[...] * branch).astype(o_ref.dtype)


def _sgu(x, layer, g, w_in, ln_g, ln_b, w_s, b_s, p_b):
    n, d = x.shape
    tm = MIX_TM
    const2 = lambda i: (0, 0)
    first = (HYENA_ORDER + 1) * D_HYENA // W_IN_BLOCK
    gate_b = first + 2 * D_SGU // W_IN_BLOCK + d // W_IN_BLOCK
    w_blk = lambda blk: _resident((None, d, W_IN_BLOCK), lambda i: (layer, 0, blk))
    return pl.pallas_call(
        _sgu_kernel,
        grid=(n // tm,),
        in_specs=[
            pl.BlockSpec((tm, d), lambda i: (i, 0)),
            pl.BlockSpec((1, d), const2),
            w_blk(first), w_blk(first + 1), w_blk(gate_b), w_blk(gate_b + 1),
            pl.BlockSpec(ln_g.shape, const2),
            pl.BlockSpec(ln_b.shape, const2),
            _resident((None,) + w_s.shape[1:], lambda i: (layer, 0, 0, 0)),
            _resident(b_s.shape, lambda i: (0, 0, 0)),
            _resident((None,) + p_b.shape[1:], lambda i: (layer, 0, 0)),
        ],
        out_specs=pl.BlockSpec((tm, d), lambda i: (i, 0)),
        out_shape=jax.ShapeDtypeStruct((n, d), BF16),
        scratch_shapes=[pltpu.VMEM((tm, D_SGU), F32), pltpu.VMEM((tm, d), F32)],
        compiler_params=_params("parallel"),
        name="sgu_branch",
    )(x, g, w_in, w_in, w_in, w_in, ln_g, ln_b, w_s, b_s, p_b)


def _merge_kernel(x_ref, g_ref, wg0_ref, wg1_ref, a_ref, pa_ref, mb_ref, wout_ref, post_g_ref, o_ref):
    x = x_ref[...]
    xn = (_rms(x) * g_ref[...]).astype(BF16)
    pa = _dot(a_ref[...], pa_ref[...])
    halves = []
    for h, wg_ref in enumerate((wg0_ref, wg1_ref)):
        cols = slice(h * W_IN_BLOCK, (h + 1) * W_IN_BLOCK)
        gate = jax.nn.sigmoid(_dot(xn, wg_ref[...]))
        halves.append((gate * pa[:, cols] + mb_ref[:, cols].astype(F32)).astype(BF16))
    y = _dot(jnp.concatenate(halves, axis=1), wout_ref[...])
    o_ref[...] = x + _rms(y) * post_g_ref[...]


def _merge(x, layer, g, w_in, a, p_a, mb, w_out, post_g):
    n, d = x.shape
    tm = MIX_TM
    const2 = lambda i: (0, 0)
    gate_a = ((HYENA_ORDER + 1) * D_HYENA + 2 * D_SGU) // W_IN_BLOCK
    w_blk = lambda blk: _resident((None, d, W_IN_BLOCK), lambda i: (layer, 0, blk))
    return pl.pallas_call(
        _merge_kernel,
        grid=(n // tm,),
        in_specs=[
            pl.BlockSpec((tm, d), lambda i: (i, 0)),
            pl.BlockSpec((1, d), const2),
            w_blk(gate_a), w_blk(gate_a + 1),
            pl.BlockSpec((tm, a.shape[1]), lambda i: (i, 0)),
            _resident((None,) + p_a.shape[1:], lambda i: (layer, 0, 0)),
            pl.BlockSpec((tm, d), lambda i: (i, 0)),
            _resident((None,) + w_out.shape[1:], lambda i: (layer, 0, 0)),
            pl.BlockSpec((1, d), const2),
        ],
        out_specs=pl.BlockSpec((tm, d), lambda i: (i, 0)),
        out_shape=jax.ShapeDtypeStruct((n, d), F32),
        compiler_params=_params("parallel"),
        name="merge",
    )(x, g, w_in, w_in, a, p_a, mb, w_out, post_g)


def _filter_positions(seq):
    t = jnp.linspace(0.0, 1.0, seq, dtype=F32)[:, None]
    w = (2.0 * math.pi / seq) * jnp.arange(seq, dtype=F32)[:, None]
    f = jnp.linspace(1e-4, FILTER_BANDS - 1, FILTER_BANDS, dtype=F32)[None, :]
    z = jnp.concatenate([t, jnp.cos(f * w), -jnp.sin(f * w)], axis=-1)
    return jnp.pad(z, ((0, 0), (0, LANES - FILTER_EMB)))


def _decay_rates():
    max_decay = math.log(DECAY_TARGET) / DECAY_FAST_PCT
    min_decay = math.log(DECAY_TARGET) / DECAY_SLOW_PCT
    return jnp.abs(jnp.linspace(min_decay, max_decay, D_HYENA, dtype=F32))[None, :]


def _fft_tables(seq):
    n = 2 * seq
    n2 = FFT_N2
    n1 = n // n2
    h1 = n1 // 2
    k1 = jnp.arange(h1, dtype=jnp.int32)
    m1 = ((2 * k1[:, None] + 1) * k1[None, :]) % (2 * n1)
    a1 = m1.astype(F32) * (math.pi / n1)
    f1 = jnp.stack([jnp.cos(a1), -jnp.sin(a1)], axis=1).reshape(n1, h1)
    q = jnp.arange(n2, dtype=jnp.int32)
    kk = 2 * (k1[:, None, None] + n1 * q[None, :, None]) + 1
    a2 = ((kk * q[None, None, :]) % (2 * n)).astype(F32) * (math.pi / n)
    c, s = jnp.cos(a2), jnp.sin(a2)
    t2 = jnp.concatenate([jnp.concatenate([c, s], axis=2), jnp.concatenate([-s, c], axis=2)], axis=1)
    eye = jnp.eye(FFT_ROWS, dtype=F32)
    g1, g1t = jnp.kron(f1, eye), jnp.kron(f1.T, eye)
    return (g1.astype(BF16), g1t.astype(BF16), t2.astype(BF16), jnp.swapaxes(t2, 1, 2).astype(BF16))


def kernel(x, ffn1_pre_g, ffn1_w_gate, ffn1_w_up, ffn1_w_down, ffn1_post_g, mix_pre_g, w_in, hy_conv_w, hy_conv_b, filt_w1, filt_b, filt_freq, filt_w_inner, filt_w_out, hy_skip, sgu_ln_g, sgu_ln_b, sgu_w_s, sgu_b_s, p_a, p_b, w_out, mix_post_g, ffn2_pre_g, ffn2_w_gate, ffn2_w_up, ffn2_w_down, ffn2_post_g):
    bsz, seq, d = x.shape
    depth = w_in.shape[0]
    n_tok = bsz * seq

    z = _filter_positions(seq)
    abs_delta = _decay_rates()
    g1, g1t, t2, t2t = _fft_tables(seq)
    n2 = FFT_N2
    h1 = seq // n2
    n1 = 2 * h1
    nsig = HYENA_ORDER * N_DIRS
    row = lambda v: v[None, :]
    bf = lambda w: w.astype(BF16)

    ffn1 = (ffn1_w_gate, ffn1_w_up, ffn1_w_down)
    ffn2 = (ffn2_w_gate, ffn2_w_up, ffn2_w_down)
    w_in, p_a, p_b, w_out, sgu_w_s = bf(w_in), bf(p_a), bf(p_b), bf(w_out), bf(sgu_w_s)

    xs = x.reshape(n_tok, d)
    for i in range(depth):
        xs = _ffn(xs, i, row(ffn1_pre_g[i]), *ffn1, row(ffn1_post_g[i]))
        pre_g = row(mix_pre_g[i])

        u = _proj_conv(xs, i, pre_g, w_in, hy_conv_w[i], row(hy_conv_b[i]), seq)
        u = u.reshape(HYENA_ORDER + 1, bsz, h1, n2, D_HYENA)
        w1 = jnp.pad(filt_w1[i], ((0, LANES - FILTER_EMB), (0, 0)))
        taps_a = _filter_stage1(z.reshape(h1, n2, LANES), w1, filt_b[i], filt_freq[i], filt_w_inner[i],
                                filt_w_out[i], abs_delta, g1)
        k_spec = _filter_spectrum(t2, taps_a.reshape(nsig, h1, 2 * n2, D_HYENA), seq)
        y, y_part = u, 0
        a1 = _stage1(g1, y, y_part)
        for o in range(HYENA_ORDER):
            last = o == HYENA_ORDER - 1
            b2 = _spectral(t2, t2t, a1.reshape(bsz, h1, 2 * n2, D_HYENA), k_spec, o)
            outs = _inverse_gate(g1t, b2.reshape(bsz, n1, n2, D_HYENA), u, o + 1, y, y_part,
                                 row(hy_skip[i, o]), g1=None if last else g1)
            y, y_part = outs[0], 0
            a1 = None if last else outs[1]
        a = y.reshape(n_tok, D_HYENA)

        b_s = jnp.broadcast_to(sgu_b_s[i][:, :, None], (SGU_GROUPS, SGU_CHUNK, D_SGU // SGU_GROUPS))
        mb = _sgu(xs, i, pre_g, w_in, row(sgu_ln_g[i]), row(sgu_ln_b[i]), sgu_w_s, b_s, p_b)

        xs = _merge(xs, i, pre_g, w_in, a, p_a, mb, w_out, row(mix_post_g[i]))
        xs = _ffn(xs, i, row(ffn2_pre_g[i]), *ffn2, row(ffn2_post_g[i]))
    return xs.reshape(bsz, seq, d)
```

```python
import functools
import math

import jax
import jax.numpy as jnp
from jax import lax
from jax.experimental import pallas as pl
from jax.experimental.pallas import tpu as pltpu

F32 = jnp.float32
BF16 = jnp.bfloat16

D_MODEL = 2048
D_HYENA = D_MODEL // 2
HYENA_ORDER = 2
SHORT_CONV = 3
FILTER_BANDS = 16
FILTER_EMB = 1 + 2 * FILTER_BANDS
FILTER_INNER = 2
DECAY_FAST_PCT = 0.3
DECAY_SLOW_PCT = 1.5
DECAY_TARGET = 1e-2
N_DIRS = 2
D_SGU = D_MODEL // 2
SGU_CHUNK = 128
SGU_GROUPS = 8
NORM_EPS = 1e-6
LN_EPS = 1e-5

LANES = 128
SUBLANES = 8
VMEM_LIMIT_BYTES = 56 * 1024 * 1024
FFN_VMEM_LIMIT_BYTES = 61 * 1024 * 1024

FFN_TM = 1024
FFN_TF = 256
PROJ_TM = 512
MIX_TM = 512
FFT_N2 = 128
FFT_ROWS = 16
FFT_GROUPS = 2
SPECTRAL_K1 = 2
ROW_CHUNK = 16
ROW_UNROLL = True
W_IN_BLOCK = 1024


def _params(*sem, vmem=VMEM_LIMIT_BYTES):
    return pltpu.CompilerParams(dimension_semantics=sem, vmem_limit_bytes=vmem)


def _resident(shape, index_map):
    return pl.BlockSpec(shape, index_map, pipeline_mode=pl.Buffered(1))


def _rms(x):
    return x * lax.rsqrt(jnp.mean(x * x, axis=-1, keepdims=True) + NORM_EPS)


def _dot(a, b):
    return jnp.dot(a, b, preferred_element_type=F32)


def _row_chunks(n_rows, body, unroll=ROW_UNROLL):
    def step(r, carry):
        body(pl.ds(pl.multiple_of(r * ROW_CHUNK, ROW_CHUNK), ROW_CHUNK))
        return carry
    lax.fori_loop(0, n_rows // ROW_CHUNK, step, 0, unroll=unroll)


def _rms_scales(src_ref, scale_ref):
    def body(rows):
        a = src_ref[rows, :]
        scale_ref[rows, :] = lax.rsqrt(jnp.mean(a * a, axis=-1, keepdims=True) + NORM_EPS)
    _row_chunks(src_ref.shape[0], body)


def _ffn_kernel(x_ref, pre_g_ref, wg_ref, wu_ref, wd_ref, post_g_ref, o_ref, xn_ref, acc_ref, scale_ref):
    j = pl.program_id(1)
    tm = x_ref.shape[0]

    @pl.when(j == 0)
    def _():
        _rms_scales(x_ref, scale_ref)
        gain = pre_g_ref[...]

        def body(rows):
            xn_ref[rows, :] = (x_ref[rows, :] * scale_ref[rows, :] * gain).astype(BF16)
        _row_chunks(tm, body)

    def step(first):
        xn = xn_ref[...]
        g = _dot(xn, wg_ref[...].astype(BF16))
        u = _dot(xn, wu_ref[...].astype(BF16))
        h = (g * jax.nn.sigmoid(g) * u).astype(BF16)
        d = _dot(h, wd_ref[...].astype(BF16))
        acc_ref[...] = d if first else acc_ref[...] + d

    pl.when(j == 0)(functools.partial(step, True))
    pl.when(j > 0)(functools.partial(step, False))

    @pl.when(j == pl.num_programs(1) - 1)
    def _():
        _rms_scales(acc_ref, scale_ref)
        half_gain = 0.5 * post_g_ref[...]

        def body(rows):
            o_ref[rows, :] = x_ref[rows, :] + acc_ref[rows, :] * scale_ref[rows, :] * half_gain
        _row_chunks(tm, body)


def _ffn(x, layer, pre_g, wg, wu, wd, post_g):
    n, d = x.shape
    f = wg.shape[2]
    tm, tf = FFN_TM, FFN_TF
    return pl.pallas_call(
        _ffn_kernel,
        grid=(n // tm, f // tf),
        in_specs=[
            pl.BlockSpec((tm, d), lambda i, j: (i, 0)),
            pl.BlockSpec((1, d), lambda i, j: (0, 0)),
            pl.BlockSpec((None, d, tf), lambda i, j: (layer, 0, j)),
            pl.BlockSpec((None, d, tf), lambda i, j: (layer, 0, j)),
            pl.BlockSpec((None, tf, d), lambda i, j: (layer, j, 0)),
            pl.BlockSpec((1, d), lambda i, j: (0, 0)),
        ],
        out_specs=pl.BlockSpec((tm, d), lambda i, j: (i, 0)),
        out_shape=jax.ShapeDtypeStruct((n, d), F32),
        scratch_shapes=[pltpu.VMEM((tm, d), BF16), pltpu.VMEM((tm, d), F32), pltpu.VMEM((tm, 1), F32)],
        compiler_params=_params("parallel", "arbitrary", vmem=FFN_VMEM_LIMIT_BYTES),
        name="ffn",
    )(x, pre_g, wg, wu, wd, post_g)


def _proj_conv_kernel(xp_ref, x_ref, xn_ref, g_ref, w_ref, cw_ref, cb_ref, o_ref, *, tiles_per_seq):
    tm = x_ref.shape[0]
    pos = lax.rem(pl.program_id(0), tiles_per_seq)
    keep_prev = (pos != 0).astype(F32)
    keep_next = (pos != tiles_per_seq - 1).astype(F32)
    xa = jnp.concatenate([xp_ref[...] * keep_prev, x_ref[...], xn_ref[...] * keep_next], axis=0)
    xn = (_rms(xa) * g_ref[...]).astype(BF16)
    rows = xn.shape[0]
    mid = slice(SUBLANES, SUBLANES + tm)
    for part in range(o_ref.shape[0]):
        cols = slice(part * D_HYENA, (part + 1) * D_HYENA)
        p = _dot(xn, w_ref[:, cols])
        prev = pltpu.roll(p, 1, 0)[mid]
        nxt = pltpu.roll(p, rows - 1, 0)[mid]
        cw = cw_ref[:, cols]
        y = cb_ref[:, cols] + prev * cw[0:1] + p[mid] * cw[1:2] + nxt * cw[2:3]
        o_ref[part] = y.astype(o_ref.dtype)


def _proj_conv(x, layer, g, w_in, conv_w, conv_b, seq):
    n, d = x.shape
    parts = HYENA_ORDER + 1
    c = parts * D_HYENA
    tm = PROJ_TM
    halo_per_tile = tm // SUBLANES
    last_halo = n // SUBLANES - 1
    const2 = lambda i: (0, 0)
    return pl.pallas_call(
        functools.partial(_proj_conv_kernel, tiles_per_seq=seq // tm),
        grid=(n // tm,),
        in_specs=[
            pl.BlockSpec((SUBLANES, d), lambda i: (jnp.maximum(i * halo_per_tile - 1, 0), 0)),
            pl.BlockSpec((tm, d), lambda i: (i, 0)),
            pl.BlockSpec((SUBLANES, d), lambda i: (jnp.minimum((i + 1) * halo_per_tile, last_halo), 0)),
            pl.BlockSpec((1, d), const2),
            _resident((None, d, c), lambda i: (layer, 0, 0)),
            pl.BlockSpec((SHORT_CONV, c), const2),
            pl.BlockSpec((1, c), const2),
        ],
        out_specs=pl.BlockSpec((parts, tm, D_HYENA), lambda i: (0, i, 0)),
        out_shape=jax.ShapeDtypeStruct((parts, n, D_HYENA), BF16),
        compiler_params=_params("parallel"),
        name="hyena_proj_conv",
    )(x, x, x, g, w_in, conv_w, conv_b)


def _filter_kernel(z_ref, w1_ref, b_ref, freq_ref, wi_ref, wo_ref, delta_ref, g_ref, o_ref):
    hp = functools.partial(jnp.dot, preferred_element_type=F32, precision=lax.Precision.HIGHEST)
    h1, r, feat = z_ref.shape
    z = z_ref[...].reshape(h1 * r, feat)
    b = b_ref[...]
    freq = freq_ref[...]
    h = jnp.sin(freq[0:1] * (hp(z, w1_ref[...]) + b[0:1]))
    for j in range(FILTER_INNER):
        h = jnp.sin(freq[j + 1:j + 2] * (hp(h, wi_ref[j]) + b[j + 1:j + 2]))
    h = _dot(h.astype(BF16), wo_ref[...].astype(BF16))
    t = z[:, 0:1]
    decay = jnp.exp(-t * delta_ref[...])
    flat = lax.broadcasted_iota(jnp.int32, (h1 * r, D_HYENA), 0)
    lag0 = jnp.logical_and(flat == 0, pl.program_id(0) == 0)
    n1 = o_ref.shape[1]
    for blk in range(HYENA_ORDER * N_DIRS):
        hb = h[:, blk * D_HYENA:(blk + 1) * D_HYENA] * decay
        if blk % N_DIRS == 1:
            hb = jnp.where(lag0, 0.0, hb)
        a = _dot(g_ref[...], hb.astype(BF16))
        o_ref[blk] = a.astype(o_ref.dtype).reshape(n1, r, D_HYENA)


def _filter_stage1(z, w1, b, freq, w_inner, w_out, abs_delta, g1):
    h1, n2, feat = z.shape
    nsig = HYENA_ORDER * N_DIRS
    r = FFT_ROWS
    n1 = g1.shape[0] // r
    full = lambda a: pl.BlockSpec(a.shape, lambda i: (0,) * a.ndim)
    return pl.pallas_call(
        _filter_kernel,
        grid=(n2 // r,),
        in_specs=[pl.BlockSpec((h1, r, feat), lambda i: (0, i, 0)),
                  full(w1), full(b), full(freq), full(w_inner), full(w_out), full(abs_delta), full(g1)],
        out_specs=pl.BlockSpec((nsig, n1, r, D_HYENA), lambda i: (0, 0, i, 0)),
        out_shape=jax.ShapeDtypeStruct((nsig, n1, n2, D_HYENA), BF16),
        compiler_params=_params("parallel"),
        name="filter_taps",
    )(z, w1, b, freq, w_inner, w_out, abs_delta, g1)


def _slab_groups(rows_per_block):
    return [slice(g * FFT_ROWS, (g + 1) * FFT_ROWS) for g in range(rows_per_block // FFT_ROWS)]


def _stage1_kernel(g_ref, x_ref, o_ref):
    h1, rb, c = x_ref.shape[1:]
    n1 = o_ref.shape[1]
    for rows in _slab_groups(rb):
        x = x_ref[0, :, rows, :].reshape(h1 * FFT_ROWS, c)
        o_ref[0, :, rows, :] = _dot(g_ref[...], x).astype(o_ref.dtype).reshape(n1, FFT_ROWS, c)


def _stage1(g1, x, part):
    _, nsig, h1, n2, c = x.shape
    r = FFT_ROWS * FFT_GROUPS
    n1 = g1.shape[0] // FFT_ROWS
    return pl.pallas_call(
        _stage1_kernel,
        grid=(nsig, n2 // r),
        in_specs=[
            pl.BlockSpec(g1.shape, lambda s, j: (0, 0)),
            pl.BlockSpec((None, 1, h1, r, c), lambda s, j: (part, s, 0, j, 0)),
        ],
        out_specs=pl.BlockSpec((1, n1, r, c), lambda s, j: (s, 0, j, 0)),
        out_shape=jax.ShapeDtypeStruct((nsig, n1, n2, c), BF16),
        compiler_params=_params("parallel", "parallel"),
        name="fft_stage1",
    )(g1, x)


def _filter_spectrum_kernel(t_ref, a_ref, k_ref, *, scale):
    t = t_ref[0]
    half = t.shape[0] // 2
    for o in range(HYENA_ORDER):
        xf = _dot(t, a_ref[N_DIRS * o, 0])
        xb = _dot(t, a_ref[N_DIRS * o + 1, 0])
        k_ref[o, 0, :half, :] = ((xf[:half] + xb[:half]) * scale).astype(k_ref.dtype)
        k_ref[o, 0, half:, :] = ((xf[half:] - xb[half:]) * scale).astype(k_ref.dtype)


def _filter_spectrum(t2, a, seq):
    nsig, h1, rows, c = a.shape
    return pl.pallas_call(
        functools.partial(_filter_spectrum_kernel, scale=1.0 / seq),
        grid=(h1,),
        in_specs=[
            pl.BlockSpec((1, rows, rows), lambda k: (k, 0, 0)),
            pl.BlockSpec((nsig, 1, rows, c), lambda k: (0, k, 0, 0)),
        ],
        out_specs=pl.BlockSpec((HYENA_ORDER, 1, rows, c), lambda k: (0, k, 0, 0)),
        out_shape=jax.ShapeDtypeStruct((HYENA_ORDER, h1, rows, c), BF16),
        compiler_params=_params("parallel"),
        name="filter_spectrum",
    )(t2, a)


def _spectral_kernel(t_ref, tt_ref, a_ref, k_ref, o_ref):
    for q in range(t_ref.shape[0]):
        t = t_ref[q]
        tt = tt_ref[q]
        k = k_ref[0, q].astype(F32)
        half = k.shape[0] // 2
        kr, ki = k[:half], k[half:]
        for b in range(a_ref.shape[0]):
            x = _dot(t, a_ref[b, q])
            xr, xi = x[:half], x[half:]
            z = jnp.concatenate([xr * kr - xi * ki, xr * ki + xi * kr], axis=0).astype(BF16)
            o_ref[b, q] = _dot(tt, z).astype(o_ref.dtype)


def _spectral(t2, t2t, a, k, order):
    bsz, h1, rows, c = a.shape
    kb = SPECTRAL_K1
    return pl.pallas_call(
        _spectral_kernel,
        grid=(h1 // kb,),
        in_specs=[
            pl.BlockSpec((kb, rows, rows), lambda q: (q, 0, 0)),
            pl.BlockSpec((kb, rows, rows), lambda q: (q, 0, 0)),
            pl.BlockSpec((bsz, kb, rows, c), lambda q: (0, q, 0, 0)),
            pl.BlockSpec((1, kb, rows, c), lambda q: (order, q, 0, 0)),
        ],
        out_specs=pl.BlockSpec((bsz, kb, rows, c), lambda q: (0, q, 0, 0)),
        out_shape=jax.ShapeDtypeStruct((bsz, h1, rows, c), BF16),
        compiler_params=_params("parallel"),
        name="hyena_spectral",
    )(t2, t2t, a, k)


def _inverse_gate_kernel(gt_ref, b_ref, gate_ref, y_ref, skip_ref, *rest, emit_stage1):
    if emit_stage1:
        g_ref, o_ref, a_ref = rest
    else:
        (o_ref,), a_ref = rest, None
    n1, rb, c = b_ref.shape[1:]
    h1 = o_ref.shape[2]
    for rows in _slab_groups(rb):
        conv = _dot(gt_ref[...], b_ref[0, :, rows, :].reshape(n1 * FFT_ROWS, c))
        y = y_ref[0, :, rows, :].reshape(conv.shape).astype(F32)
        gate = gate_ref[0, :, rows, :].reshape(conv.shape).astype(F32)
        out = (gate * (conv + y * skip_ref[...])).astype(BF16)
        o_ref[0, 0, :, rows, :] = out.reshape(h1, FFT_ROWS, c)
        if a_ref is not None:
            a_ref[0, :, rows, :] = _dot(g_ref[...], out).astype(BF16).reshape(n1, FFT_ROWS, c)


def _inverse_gate(g1t, b, gate, gate_part, y, y_part, skip, g1=None):
    bsz, n1, n2, c = b.shape
    r = FFT_ROWS * FFT_GROUPS
    h1 = g1t.shape[0] // FFT_ROWS
    const2 = lambda s, j: (0, 0)
    in_specs = [
        pl.BlockSpec(g1t.shape, const2),
        pl.BlockSpec((1, n1, r, c), lambda s, j: (s, 0, j, 0)),
        pl.BlockSpec((None, 1, h1, r, c), lambda s, j: (gate_part, s, 0, j, 0)),
        pl.BlockSpec((None, 1, h1, r, c), lambda s, j: (y_part, s, 0, j, 0)),
        pl.BlockSpec((1, c), const2),
    ]
    out_specs = [pl.BlockSpec((1, 1, h1, r, c), lambda s, j: (0, s, 0, j, 0))]
    out_shape = [jax.ShapeDtypeStruct((1, bsz, h1, n2, c), BF16)]
    args = [g1t, b, gate, y, skip]
    if g1 is not None:
        in_specs.append(pl.BlockSpec(g1.shape, const2))
        out_specs.append(pl.BlockSpec((1, n1, r, c), lambda s, j: (s, 0, j, 0)))
        out_shape.append(jax.ShapeDtypeStruct((bsz, n1, n2, c), BF16))
        args.append(g1)
    return pl.pallas_call(
        functools.partial(_inverse_gate_kernel, emit_stage1=g1 is not None),
        grid=(bsz, n2 // r),
        in_specs=in_specs,
        out_specs=out_specs,
        out_shape=out_shape,
        compiler_params=_params("parallel", "parallel"),
        name="hyena_inverse_gate",
    )(*args)


def _sgu_kernel(x_ref, g_ref, wu_ref, wv_ref, wg0_ref, wg1_ref, lng_ref, lnb_ref, ws_ref, bs_ref,
                pb_ref, o_ref, s_ref, gate_ref):
    xn = (_rms(x_ref[...]) * g_ref[...]).astype(BF16)
    v = jax.nn.gelu(_dot(xn, wv_ref[...]), approximate=True)
    mu = jnp.mean(v, axis=-1, keepdims=True)
    vc = v - mu
    var = jnp.mean(vc * vc, axis=-1, keepdims=True)
    v = (vc * lax.rsqrt(var + LN_EPS) * lng_ref[...] + lnb_ref[...]).astype(BF16)
    for h, wg_ref in enumerate((wg0_ref, wg1_ref)):
        gate_ref[:, h * W_IN_BLOCK:(h + 1) * W_IN_BLOCK] = jax.nn.sigmoid(_dot(xn, wg_ref[...]))
    u = jax.nn.gelu(_dot(xn, wu_ref[...]), approximate=True)
    dg = D_SGU // SGU_GROUPS
    for c in range(x_ref.shape[0] // SGU_CHUNK):
        rows = slice(c * SGU_CHUNK, (c + 1) * SGU_CHUNK)
        for grp in range(SGU_GROUPS):
            cols = slice(grp * dg, (grp + 1) * dg)
            s_ref[rows, cols] = _dot(ws_ref[grp], v[rows, cols]) + bs_ref[grp]
    gated = (u * s_ref[...]).astype(BF16)
    branch = _dot(gated, pb_ref[...])
    o_ref[...] = (gate_ref[...] * branch).astype(o_ref.dtype)


def _sgu(x, layer, g, w_in, ln_g, ln_b, w_s, b_s, p_b):
    n, d = x.shape
    tm = MIX_TM
    const2 = lambda i: (0, 0)
    first = (HYENA_ORDER + 1) * D_HYENA // W_IN_BLOCK
    gate_b = first + 2 * D_SGU // W_IN_BLOCK + d // W_IN_BLOCK
    w_blk = lambda blk: _resident((None, d, W_IN_BLOCK), lambda i: (layer, 0, blk))
    return pl.pallas_call(
        _sgu_kernel,
        grid=(n // tm,),
        in_specs=[
            pl.BlockSpec((tm, d), lambda i: (i, 0)),
            pl.BlockSpec((1, d), const2),
            w_blk(first), w_blk(first + 1), w_blk(gate_b), w_blk(gate_b + 1),
            pl.BlockSpec(ln_g.shape, const2),
            pl.BlockSpec(ln_b.shape, const2),
            _resident((None,) + w_s.shape[1:], lambda i: (layer, 0, 0, 0)),
            _resident(b_s.shape, lambda i: (0, 0, 0)),
            _resident((None,) + p_b.shape[1:], lambda i: (layer, 0, 0)),
        ],
        out_specs=pl.BlockSpec((tm, d), lambda i: (i, 0)),
        out_shape=jax.ShapeDtypeStruct((n, d), BF16),
        scratch_shapes=[pltpu.VMEM((tm, D_SGU), F32), pltpu.VMEM((tm, d), F32)],
        compiler_params=_params("parallel"),
        name="sgu_branch",
    )(x, g, w_in, w_in, w_in, w_in, ln_g, ln_b, w_s, b_s, p_b)


def _merge_kernel(x_ref, g_ref, wg0_ref, wg1_ref, a_ref, pa_ref, mb_ref, wout_ref, post_g_ref, o_ref):
    x = x_ref[...]
    xn = (_rms(x) * g_ref[...]).astype(BF16)
    pa = _dot(a_ref[...], pa_ref[...])
    halves = []
    for h, wg_ref in enumerate((wg0_ref, wg1_ref)):
        cols = slice(h * W_IN_BLOCK, (h + 1) * W_IN_BLOCK)
        gate = jax.nn.sigmoid(_dot(xn, wg_ref[...]))
        halves.append((gate * pa[:, cols] + mb_ref[:, cols].astype(F32)).astype(BF16))
    y = _dot(jnp.concatenate(halves, axis=1), wout_ref[...])
    o_ref[...] = x + _rms(y) * post_g_ref[...]


def _merge(x, layer, g, w_in, a, p_a, mb, w_out, post_g):
    n, d = x.shape
    tm = MIX_TM
    const2 = lambda i: (0, 0)
    gate_a = ((HYENA_ORDER + 1) * D_HYENA + 2 * D_SGU) // W_IN_BLOCK
    w_blk = lambda blk: _resident((None, d, W_IN_BLOCK), lambda i: (layer, 0, blk))
    return pl.pallas_call(
        _merge_kernel,
        grid=(n // tm,),
        in_specs=[
            pl.BlockSpec((tm, d), lambda i: (i, 0)),
            pl.BlockSpec((1, d), const2),
            w_blk(gate_a), w_blk(gate_a + 1),
            pl.BlockSpec((tm, a.shape[1]), lambda i: (i, 0)),
            _resident((None,) + p_a.shape[1:], lambda i: (layer, 0, 0)),
            pl.BlockSpec((tm, d), lambda i: (i, 0)),
            _resident((None,) + w_out.shape[1:], lambda i: (layer, 0, 0)),
            pl.BlockSpec((1, d), const2),
        ],
        out_specs=pl.BlockSpec((tm, d), lambda i: (i, 0)),
        out_shape=jax.ShapeDtypeStruct((n, d), F32),
        compiler_params=_params("parallel"),
        name="merge",
    )(x, g, w_in, w_in, a, p_a, mb, w_out, post_g)


def _filter_positions(seq):
    t = jnp.linspace(0.0, 1.0, seq, dtype=F32)[:, None]
    w = (2.0 * math.pi / seq) * jnp.arange(seq, dtype=F32)[:, None]
    f = jnp.linspace(1e-4, FILTER_BANDS - 1, FILTER_BANDS, dtype=F32)[None, :]
    z = jnp.concatenate([t, jnp.cos(f * w), -jnp.sin(f * w)], axis=-1)
    return jnp.pad(z, ((0, 0), (0, LANES - FILTER_EMB)))


def _decay_rates():
    max_decay = math.log(DECAY_TARGET) / DECAY_FAST_PCT
    min_decay = math.log(DECAY_TARGET) / DECAY_SLOW_PCT
    return jnp.abs(jnp.linspace(min_decay, max_decay, D_HYENA, dtype=F32))[None, :]


def _fft_tables(seq):
    n = 2 * seq
    n2 = FFT_N2
    n1 = n // n2
    h1 = n1 // 2
    k1 = jnp.arange(h1, dtype=jnp.int32)
    m1 = ((2 * k1[:, None] + 1) * k1[None, :]) % (2 * n1)
    a1 = m1.astype(F32) * (math.pi / n1)
    f1 = jnp.stack([jnp.cos(a1), -jnp.sin(a1)], axis=1).reshape(n1, h1)
    q = jnp.arange(n2, dtype=jnp.int32)
    kk = 2 * (k1[:, None, None] + n1 * q[None, :, None]) + 1
    a2 = ((kk * q[None, None, :]) % (2 * n)).astype(F32) * (math.pi / n)
    c, s = jnp.cos(a2), jnp.sin(a2)
    t2 = jnp.concatenate([jnp.concatenate([c, s], axis=2), jnp.concatenate([-s, c], axis=2)], axis=1)
    eye = jnp.eye(FFT_ROWS, dtype=F32)
    g1, g1t = jnp.kron(f1, eye), jnp.kron(f1.T, eye)
    return (g1.astype(BF16), g1t.astype(BF16), t2.astype(BF16), jnp.swapaxes(t2, 1, 2).astype(BF16))


def kernel(x, ffn1_pre_g, ffn1_w_gate, ffn1_w_up, ffn1_w_down, ffn1_post_g, mix_pre_g, w_in, hy_conv_w, hy_conv_b, filt_w1, filt_b, filt_freq, filt_w_inner, filt_w_out, hy_skip, sgu_ln_g, sgu_ln_b, sgu_w_s, sgu_b_s, p_a, p_b, w_out, mix_post_g, ffn2_pre_g, ffn2_w_gate, ffn2_w_up, ffn2_w_down, ffn2_post_g):
    bsz, seq, d = x.shape
    depth = w_in.shape[0]
    n_tok = bsz * seq

    z = _filter_positions(seq)
    abs_delta = _decay_rates()
    g1, g1t, t2, t2t = _fft_tables(seq)
    n2 = FFT_N2
    h1 = seq // n2
    n1 = 2 * h1
    nsig = HYENA_ORDER * N_DIRS
    row = lambda v: v[None, :]
    bf = lambda w: w.astype(BF16)

    ffn1 = (ffn1_w_gate, ffn1_w_up, ffn1_w_down)
    ffn2 = (ffn2_w_gate, ffn2_w_up, ffn2_w_down)
    w_in, p_a, p_b, w_out, sgu_w_s = bf(w_in), bf(p_a), bf(p_b), bf(w_out), bf(sgu_w_s)

    xs = x.reshape(n_tok, d)
    for i in range(depth):
        xs = _ffn(xs, i, row(ffn1_pre_g[i]), *ffn1, row(ffn1_post_g[i]))
        pre_g = row(mix_pre_g[i])

        u = _proj_conv(xs, i, pre_g, w_in, hy_conv_w[i], row(hy_conv_b[i]), seq)
        u = u.reshape(HYENA_ORDER + 1, bsz, h1, n2, D_HYENA)
        w1 = jnp.pad(filt_w1[i], ((0, LANES - FILTER_EMB), (0, 0)))
        taps_a = _filter_stage1(z.reshape(h1, n2, LANES), w1, filt_b[i], filt_freq[i], filt_w_inner[i],
                                filt_w_out[i], abs_delta, g1)
        k_spec = _filter_spectrum(t2, taps_a.reshape(nsig, h1, 2 * n2, D_HYENA), seq)
        y, y_part = u, 0
        a1 = _stage1(g1, y, y_part)
        for o in range(HYENA_ORDER):
            last = o == HYENA_ORDER - 1
            b2 = _spectral(t2, t2t, a1.reshape(bsz, h1, 2 * n2, D_HYENA), k_spec, o)
            outs = _inverse_gate(g1t, b2.reshape(bsz, n1, n2, D_HYENA), u, o + 1, y, y_part,
                                 row(hy_skip[i, o]), g1=None if last else g1)
            y, y_part = outs[0], 0
            a1 = None if last else outs[1]
        a = y.reshape(n_tok, D_HYENA)

        b_s = jnp.broadcast_to(sgu_b_s[i][:, :, None], (SGU_GROUPS, SGU_CHUNK, D_SGU // SGU_GROUPS))
        mb = _sgu(xs, i, pre_g, w_in, row(sgu_ln_g[i]), row(sgu_ln_b[i]), sgu_w_s, b_s, p_b)

        xs = _merge(xs, i, pre_g, w_in, a, p_a, mb, w_out, row(mix_post_g[i]))
        xs = _ffn(xs, i, row(ffn2_pre_g[i]), *ffn2, row(ffn2_post_g[i]))
    return xs.reshape(bsz, seq, d)
```

```python
import functools
import math

import jax
import jax.numpy as jnp
from jax import lax
from jax.experimental import pallas as pl
from jax.experimental.pallas import tpu as pltpu

F32 = jnp.float32
BF16 = jnp.bfloat16

D_MODEL = 2048
D_HYENA = D_MODEL // 2
HYENA_ORDER = 2
SHORT_CONV = 3
FILTER_BANDS = 16
FILTER_EMB = 1 + 2 * FILTER_BANDS
FILTER_INNER = 2
DECAY_FAST_PCT = 0.3
DECAY_SLOW_PCT = 1.5
DECAY_TARGET = 1e-2
N_DIRS = 2
D_SGU = D_MODEL // 2
SGU_CHUNK = 128
SGU_GROUPS = 8
NORM_EPS = 1e-6
LN_EPS = 1e-5

LANES = 128
SUBLANES = 8
VMEM_LIMIT_BYTES = 56 * 1024 * 1024
FFN_VMEM_LIMIT_BYTES = 61 * 1024 * 1024

FFN_TM = 1024
FFN_TF = 256
PROJ_TM = 1024
MIX_TM = 512
FFT_N2 = 128
FFT_ROWS = 16
FFT_GROUPS = 2
SPECTRAL_K1 = 2
ROW_CHUNK = 16
ROW_UNROLL = True
W_IN_BLOCK = 1024


def _params(*sem, vmem=VMEM_LIMIT_BYTES):
    return pltpu.CompilerParams(dimension_semantics=sem, vmem_limit_bytes=vmem)


def _resident(shape, index_map):
    return pl.BlockSpec(shape, index_map, pipeline_mode=pl.Buffered(1))


def _rms(x):
    return x * lax.rsqrt(jnp.mean(x * x, axis=-1, keepdims=True) + NORM_EPS)


def _dot(a, b):
    return jnp.dot(a, b, preferred_element_type=F32)


def _row_chunks(n_rows, body, unroll=ROW_UNROLL):
    def step(r, carry):
        body(pl.ds(pl.multiple_of(r * ROW_CHUNK, ROW_CHUNK), ROW_CHUNK))
        return carry
    lax.fori_loop(0, n_rows // ROW_CHUNK, step, 0, unroll=unroll)


def _rms_scales(src_ref, scale_ref):
    def body(rows):
        a = src_ref[rows, :]
        scale_ref[rows, :] = lax.rsqrt(jnp.mean(a * a, axis=-1, keepdims=True) + NORM_EPS)
    _row_chunks(src_ref.shape[0], body)


def _ffn_kernel(x_ref, pre_g_ref, wg_ref, wu_ref, wd_ref, post_g_ref, o_ref, xn_ref, acc_ref, scale_ref):
    j = pl.program_id(1)
    tm = x_ref.shape[0]

    @pl.when(j == 0)
    def _():
        _rms_scales(x_ref, scale_ref)
        gain = pre_g_ref[...]

        def body(rows):
            xn_ref[rows, :] = (x_ref[rows, :] * scale_ref[rows, :] * gain).astype(BF16)
        _row_chunks(tm, body)

    def step(first):
        xn = xn_ref[...]
        g = _dot(xn, wg_ref[...].astype(BF16))
        u = _dot(xn, wu_ref[...].astype(BF16))
        h = (g * jax.nn.sigmoid(g) * u).astype(BF16)
        d = _dot(h, wd_ref[...].astype(BF16))
        acc_ref[...] = d if first else acc_ref[...] + d

    pl.when(j == 0)(functools.partial(step, True))
    pl.when(j > 0)(functools.partial(step, False))

    @pl.when(j == pl.num_programs(1) - 1)
    def _():
        _rms_scales(acc_ref, scale_ref)
        half_gain = 0.5 * post_g_ref[...]

        def body(rows):
            o_ref[rows, :] = x_ref[rows, :] + acc_ref[rows, :] * scale_ref[rows, :] * half_gain
        _row_chunks(tm, body)


def _ffn(x, layer, pre_g, wg, wu, wd, post_g):
    n, d = x.shape
    f = wg.shape[2]
    tm, tf = FFN_TM, FFN_TF
    return pl.pallas_call(
        _ffn_kernel,
        grid=(n // tm, f // tf),
        in_specs=[
            pl.BlockSpec((tm, d), lambda i, j: (i, 0)),
            pl.BlockSpec((1, d), lambda i, j: (0, 0)),
            pl.BlockSpec((None, d, tf), lambda i, j: (layer, 0, j)),
            pl.BlockSpec((None, d, tf), lambda i, j: (layer, 0, j)),
            pl.BlockSpec((None, tf, d), lambda i, j: (layer, j, 0)),
            pl.BlockSpec((1, d), lambda i, j: (0, 0)),
        ],
        out_specs=pl.BlockSpec((tm, d), lambda i, j: (i, 0)),
        out_shape=jax.ShapeDtypeStruct((n, d), F32),
        scratch_shapes=[pltpu.VMEM((tm, d), BF16), pltpu.VMEM((tm, d), F32), pltpu.VMEM((tm, 1), F32)],
        compiler_params=_params("parallel", "arbitrary", vmem=FFN_VMEM_LIMIT_BYTES),
        name="ffn",
    )(x, pre_g, wg, wu, wd, post_g)


def _proj_conv_kernel(xp_ref, x_ref, xn_ref, g_ref, w_ref, cw_ref, cb_ref, o_ref, *, tiles_per_seq):
    tm = x_ref.shape[0]
    pos = lax.rem(pl.program_id(0), tiles_per_seq)
    keep_prev = (pos != 0).astype(F32)
    keep_next = (pos != tiles_per_seq - 1).astype(F32)
    xa = jnp.concatenate([xp_ref[...] * keep_prev, x_ref[...], xn_ref[...] * keep_next], axis=0)
    xn = (_rms(xa) * g_ref[...]).astype(BF16)
    rows = xn.shape[0]
    mid = slice(SUBLANES, SUBLANES + tm)
    for part in range(o_ref.shape[0]):
        cols = slice(part * D_HYENA, (part + 1) * D_HYENA)
        p = _dot(xn, w_ref[:, cols])
        prev = pltpu.roll(p, 1, 0)[mid]
        nxt = pltpu.roll(p, rows - 1, 0)[mid]
        cw = cw_ref[:, cols]
        y = cb_ref[:, cols] + prev * cw[0:1] + p[mid] * cw[1:2] + nxt * cw[2:3]
        o_ref[part] = y.astype(o_ref.dtype)


def _proj_conv(x, layer, g, w_in, conv_w, conv_b, seq):
    n, d = x.shape
    parts = HYENA_ORDER + 1
    c = parts * D_HYENA
    tm = PROJ_TM
    halo_per_tile = tm // SUBLANES
    last_halo = n // SUBLANES - 1
    const2 = lambda i: (0, 0)
    return pl.pallas_call(
        functools.partial(_proj_conv_kernel, tiles_per_seq=seq // tm),
        grid=(n // tm,),
        in_specs=[
            pl.BlockSpec((SUBLANES, d), lambda i: (jnp.maximum(i * halo_per_tile - 1, 0), 0)),
            pl.BlockSpec((tm, d), lambda i: (i, 0)),
            pl.BlockSpec((SUBLANES, d), lambda i: (jnp.minimum((i + 1) * halo_per_tile, last_halo), 0)),
            pl.BlockSpec((1, d), const2),
            _resident((None, d, c), lambda i: (layer, 0, 0)),
            pl.BlockSpec((SHORT_CONV, c), const2),
            pl.BlockSpec((1, c), const2),
        ],
        out_specs=pl.BlockSpec((parts, tm, D_HYENA), lambda i: (0, i, 0)),
        out_shape=jax.ShapeDtypeStruct((parts, n, D_HYENA), BF16),
        compiler_params=_params("parallel"),
        name="hyena_proj_conv",
    )(x, x, x, g, w_in, conv_w, conv_b)


def _filter_kernel(z_ref, w1_ref, b_ref, freq_ref, wi_ref, wo_ref, delta_ref, g_ref, o_ref):
    hp = functools.partial(jnp.dot, preferred_element_type=F32, precision=lax.Precision.HIGHEST)
    h1, r, feat = z_ref.shape
    z = z_ref[...].reshape(h1 * r, feat)
    b = b_ref[...]
    freq = freq_ref[...]
    h = jnp.sin(freq[0:1] * (hp(z, w1_ref[...]) + b[0:1]))
    for j in range(FILTER_INNER):
        h = jnp.sin(freq[j + 1:j + 2] * (hp(h, wi_ref[j]) + b[j + 1:j + 2]))
    h = _dot(h.astype(BF16), wo_ref[...].astype(BF16))
    t = z[:, 0:1]
    decay = jnp.exp(-t * delta_ref[...])
    flat = lax.broadcasted_iota(jnp.int32, (h1 * r, D_HYENA), 0)
    lag0 = jnp.logical_and(flat == 0, pl.program_id(0) == 0)
    n1 = o_ref.shape[1]
    for blk in range(HYENA_ORDER * N_DIRS):
        hb = h[:, blk * D_HYENA:(blk + 1) * D_HYENA] * decay
        if blk % N_DIRS == 1:
            hb = jnp.where(lag0, 0.0, hb)
        a = _dot(g_ref[...], hb.astype(BF16))
        o_ref[blk] = a.astype(o_ref.dtype).reshape(n1, r, D_HYENA)


def _filter_stage1(z, w1, b, freq, w_inner, w_out, abs_delta, g1):
    h1, n2, feat = z.shape
    nsig = HYENA_ORDER * N_DIRS
    r = FFT_ROWS
    n1 = g1.shape[0] // r
    full = lambda a: pl.BlockSpec(a.shape, lambda i: (0,) * a.ndim)
    return pl.pallas_call(
        _filter_kernel,
        grid=(n2 // r,),
        in_specs=[pl.BlockSpec((h1, r, feat), lambda i: (0, i, 0)),
                  full(w1), full(b), full(freq), full(w_inner), full(w_out), full(abs_delta), full(g1)],
        out_specs=pl.BlockSpec((nsig, n1, r, D_HYENA), lambda i: (0, 0, i, 0)),
        out_shape=jax.ShapeDtypeStruct((nsig, n1, n2, D_HYENA), BF16),
        compiler_params=_params("parallel"),
        name="filter_taps",
    )(z, w1, b, freq, w_inner, w_out, abs_delta, g1)


def _slab_groups(rows_per_block):
    return [slice(g * FFT_ROWS, (g + 1) * FFT_ROWS) for g in range(rows_per_block // FFT_ROWS)]


def _stage1_kernel(g_ref, x_ref, o_ref):
    h1, rb, c = x_ref.shape[1:]
    n1 = o_ref.shape[1]
    for rows in _slab_groups(rb):
        x = x_ref[0, :, rows, :].reshape(h1 * FFT_ROWS, c)
        o_ref[0, :, rows, :] = _dot(g_ref[...], x).astype(o_ref.dtype).reshape(n1, FFT_ROWS, c)


def _stage1(g1, x, part):
    _, nsig, h1, n2, c = x.shape
    r = FFT_ROWS * FFT_GROUPS
    n1 = g1.shape[0] // FFT_ROWS
    return pl.pallas_call(
        _stage1_kernel,
        grid=(nsig, n2 // r),
        in_specs=[
            pl.BlockSpec(g1.shape, lambda s, j: (0, 0)),
            pl.BlockSpec((None, 1, h1, r, c), lambda s, j: (part, s, 0, j, 0)),
        ],
        out_specs=pl.BlockSpec((1, n1, r, c), lambda s, j: (s, 0, j, 0)),
        out_shape=jax.ShapeDtypeStruct((nsig, n1, n2, c), BF16),
        compiler_params=_params("parallel", "parallel"),
        name="fft_stage1",
    )(g1, x)


def _filter_spectrum_kernel(t_ref, a_ref, k_ref, *, scale):
    t = t_ref[0]
    half = t.shape[0] // 2
    for o in range(HYENA_ORDER):
        xf = _dot(t, a_ref[N_DIRS * o, 0])
        xb = _dot(t, a_ref[N_DIRS * o + 1, 0])
        k_ref[o, 0, :half, :] = ((xf[:half] + xb[:half]) * scale).astype(k_ref.dtype)
        k_ref[o, 0, half:, :] = ((xf[half:] - xb[half:]) * scale).astype(k_ref.dtype)


def _filter_spectrum(t2, a, seq):
    nsig, h1, rows, c = a.shape
    return pl.pallas_call(
        functools.partial(_filter_spectrum_kernel, scale=1.0 / seq),
        grid=(h1,),
        in_specs=[
            pl.BlockSpec((1, rows, rows), lambda k: (k, 0, 0)),
            pl.BlockSpec((nsig, 1, rows, c), lambda k: (0, k, 0, 0)),
        ],
        out_specs=pl.BlockSpec((HYENA_ORDER, 1, rows, c), lambda k: (0, k, 0, 0)),
        out_shape=jax.ShapeDtypeStruct((HYENA_ORDER, h1, rows, c), BF16),
        compiler_params=_params("parallel"),
        name="filter_spectrum",
    )(t2, a)


def _spectral_kernel(t_ref, tt_ref, a_ref, k_ref, o_ref):
    for q in range(t_ref.shape[0]):
        t = t_ref[q]
        tt = tt_ref[q]
        k = k_ref[0, q].astype(F32)
        half = k.shape[0] // 2
        kr, ki = k[:half], k[half:]
        for b in range(a_ref.shape[0]):
            x = _dot(t, a_ref[b, q])
            xr, xi = x[:half], x[half:]
            z = jnp.concatenate([xr * kr - xi * ki, xr * ki + xi * kr], axis=0).astype(BF16)
            o_ref[b, q] = _dot(tt, z).astype(o_ref.dtype)


def _spectral(t2, t2t, a, k, order):
    bsz, h1, rows, c = a.shape
    kb = SPECTRAL_K1
    return pl.pallas_call(
        _spectral_kernel,
        grid=(h1 // kb,),
        in_specs=[
            pl.BlockSpec((kb, rows, rows), lambda q: (q, 0, 0)),
            pl.BlockSpec((kb, rows, rows), lambda q: (q, 0, 0)),
            pl.BlockSpec((bsz, kb, rows, c), lambda q: (0, q, 0, 0)),
            pl.BlockSpec((1, kb, rows, c), lambda q: (order, q, 0, 0)),
        ],
        out_specs=pl.BlockSpec((bsz, kb, rows, c), lambda q: (0, q, 0, 0)),
        out_shape=jax.ShapeDtypeStruct((bsz, h1, rows, c), BF16),
        compiler_params=_params("parallel"),
        name="hyena_spectral",
    )(t2, t2t, a, k)


def _inverse_gate_kernel(gt_ref, b_ref, gate_ref, y_ref, skip_ref, *rest, emit_stage1):
    if emit_stage1:
        g_ref, o_ref, a_ref = rest
    else:
        (o_ref,), a_ref = rest, None
    n1, rb, c = b_ref.shape[1:]
    h1 = o_ref.shape[2]
    for rows in _slab_groups(rb):
        conv = _dot(gt_ref[...], b_ref[0, :, rows, :].reshape(n1 * FFT_ROWS, c))
        y = y_ref[0, :, rows, :].reshape(conv.shape).astype(F32)
        gate = gate_ref[0, :, rows, :].reshape(conv.shape).astype(F32)
        out = (gate * (conv + y * skip_ref[...])).astype(BF16)
        o_ref[0, 0, :, rows, :] = out.reshape(h1, FFT_ROWS, c)
        if a_ref is not None:
            a_ref[0, :, rows, :] = _dot(g_ref[...], out).astype(BF16).reshape(n1, FFT_ROWS, c)


def _inverse_gate(g1t, b, gate, gate_part, y, y_part, skip, g1=None):
    bsz, n1, n2, c = b.shape
    r = FFT_ROWS * FFT_GROUPS
    h1 = g1t.shape[0] // FFT_ROWS
    const2 = lambda s, j: (0, 0)
    in_specs = [
        pl.BlockSpec(g1t.shape, const2),
        pl.BlockSpec((1, n1, r, c), lambda s, j: (s, 0, j, 0)),
        pl.BlockSpec((None, 1, h1, r, c), lambda s, j: (gate_part, s, 0, j, 0)),
        pl.BlockSpec((None, 1, h1, r, c), lambda s, j: (y_part, s, 0, j, 0)),
        pl.BlockSpec((1, c), const2),
    ]
    out_specs = [pl.BlockSpec((1, 1, h1, r, c), lambda s, j: (0, s, 0, j, 0))]
    out_shape = [jax.ShapeDtypeStruct((1, bsz, h1, n2, c), BF16)]
    args = [g1t, b, gate, y, skip]
    if g1 is not None:
        in_specs.append(pl.BlockSpec(g1.shape, const2))
        out_specs.append(pl.BlockSpec((1, n1, r, c), lambda s, j: (s, 0, j, 0)))
        out_shape.append(jax.ShapeDtypeStruct((bsz, n1, n2, c), BF16))
        args.append(g1)
    return pl.pallas_call(
        functools.partial(_inverse_gate_kernel, emit_stage1=g1 is not None),
        grid=(bsz, n2 // r),
        in_specs=in_specs,
        out_specs=out_specs,
        out_shape=out_shape,
        compiler_params=_params("parallel", "parallel"),
        name="hyena_inverse_gate",
    )(*args)


def _sgu_kernel(x_ref, g_ref, wu_ref, wv_ref, wg0_ref, wg1_ref, lng_ref, lnb_ref, ws_ref, bs_ref,
                pb_ref, o_ref, s_ref, gate_ref):
    xn = (_rms(x_ref[...]) * g_ref[...]).astype(BF16)
    v = jax.nn.gelu(_dot(xn, wv_ref[...]), approximate=True)
    mu = jnp.mean(v, axis=-1, keepdims=True)
    vc = v - mu
    var = jnp.mean(vc * vc, axis=-1, keepdims=True)
    v = (vc * lax.rsqrt(var + LN_EPS) * lng_ref[...] + lnb_ref[...]).astype(BF16)
    for h, wg_ref in enumerate((wg0_ref, wg1_ref)):
        gate_ref[:, h * W_IN_BLOCK:(h + 1) * W_IN_BLOCK] = jax.nn.sigmoid(_dot(xn, wg_ref[...]))
    u = jax.nn.gelu(_dot(xn, wu_ref[...]), approximate=True)
    dg = D_SGU // SGU_GROUPS
    for c in range(x_ref.shape[0] // SGU_CHUNK):
        rows = slice(c * SGU_CHUNK, (c + 1) * SGU_CHUNK)
        for grp in range(SGU_GROUPS):
            cols = slice(grp * dg, (grp + 1) * dg)
            s_ref[rows, cols] = _dot(ws_ref[grp], v[rows, cols]) + bs_ref[grp]
    gated = (u * s_ref[...]).astype(BF16)
    branch = _dot(gated, pb_ref[...])
    o_ref[...] = (gate_ref[...] * branch).astype(o_ref.dtype)


def _sgu(x, layer, g, w_in, ln_g, ln_b, w_s, b_s, p_b):
    n, d = x.shape
    tm = MIX_TM
    const2 = lambda i: (0, 0)
    first = (HYENA_ORDER + 1) * D_HYENA // W_IN_BLOCK
    gate_b = first + 2 * D_SGU // W_IN_BLOCK + d // W_IN_BLOCK
    w_blk = lambda blk: _resident((None, d, W_IN_BLOCK), lambda i: (layer, 0, blk))
    return pl.pallas_call(
        _sgu_kernel,
        grid=(n // tm,),
        in_specs=[
            pl.BlockSpec((tm, d), lambda i: (i, 0)),
            pl.BlockSpec((1, d), const2),
            w_blk(first), w_blk(first + 1), w_blk(gate_b), w_blk(gate_b + 1),
            pl.BlockSpec(ln_g.shape, const2),
            pl.BlockSpec(ln_b.shape, const2),
            _resident((None,) + w_s.shape[1:], lambda i: (layer, 0, 0, 0)),
            _resident(b_s.shape, lambda i: (0, 0, 0)),
            _resident((None,) + p_b.shape[1:], lambda i: (layer, 0, 0)),
        ],
        out_specs=pl.BlockSpec((tm, d), lambda i: (i, 0)),
        out_shape=jax.ShapeDtypeStruct((n, d), BF16),
        scratch_shapes=[pltpu.VMEM((tm, D_SGU), F32), pltpu.VMEM((tm, d), F32)],
        compiler_params=_params("parallel"),
        name="sgu_branch",
    )(x, g, w_in, w_in, w_in, w_in, ln_g, ln_b, w_s, b_s, p_b)


def _merge_kernel(x_ref, g_ref, wg0_ref, wg1_ref, a_ref, pa_ref, mb_ref, wout_ref, post_g_ref, o_ref):
    x = x_ref[...]
    xn = (_rms(x) * g_ref[...]).astype(BF16)
    pa = _dot(a_ref[...], pa_ref[...])
    halves = []
    for h, wg_ref in enumerate((wg0_ref, wg1_ref)):
        cols = slice(h * W_IN_BLOCK, (h + 1) * W_IN_BLOCK)
        gate = jax.nn.sigmoid(_dot(xn, wg_ref[...]))
        halves.append((gate * pa[:, cols] + mb_ref[:, cols].astype(F32)).astype(BF16))
    y = _dot(jnp.concatenate(halves, axis=1), wout_ref[...])
    o_ref[...] = x + _rms(y) * post_g_ref[...]


def _merge(x, layer, g, w_in, a, p_a, mb, w_out, post_g):
    n, d = x.shape
    tm = MIX_TM
    const2 = lambda i: (0, 0)
    gate_a = ((HYENA_ORDER + 1) * D_HYENA + 2 * D_SGU) // W_IN_BLOCK
    w_blk = lambda blk: _resident((None, d, W_IN_BLOCK), lambda i: (layer, 0, blk))
    return pl.pallas_call(
        _merge_kernel,
        grid=(n // tm,),
        in_specs=[
            pl.BlockSpec((tm, d), lambda i: (i, 0)),
            pl.BlockSpec((1, d), const2),
            w_blk(gate_a), w_blk(gate_a + 1),
            pl.BlockSpec((tm, a.shape[1]), lambda i: (i, 0)),
            _resident((None,) + p_a.shape[1:], lambda i: (layer, 0, 0)),
            pl.BlockSpec((tm, d), lambda i: (i, 0)),
            _resident((None,) + w_out.shape[1:], lambda i: (layer, 0, 0)),
            pl.BlockSpec((1, d), const2),
        ],
        out_specs=pl.BlockSpec((tm, d), lambda i: (i, 0)),
        out_shape=jax.ShapeDtypeStruct((n, d), F32),
        compiler_params=_params("parallel"),
        name="merge",
    )(x, g, w_in, w_in, a, p_a, mb, w_out, post_g)


def _filter_positions(seq):
    t = jnp.linspace(0.0, 1.0, seq, dtype=F32)[:, None]
    w = (2.0 * math.pi / seq) * jnp.arange(seq, dtype=F32)[:, None]
    f = jnp.linspace(1e-4, FILTER_BANDS - 1, FILTER_BANDS, dtype=F32)[None, :]
    z = jnp.concatenate([t, jnp.cos(f * w), -jnp.sin(f * w)], axis=-1)
    return jnp.pad(z, ((0, 0), (0, LANES - FILTER_EMB)))


def _decay_rates():
    max_decay = math.log(DECAY_TARGET) / DECAY_FAST_PCT
    min_decay = math.log(DECAY_TARGET) / DECAY_SLOW_PCT
    return jnp.abs(jnp.linspace(min_decay, max_decay, D_HYENA, dtype=F32))[None, :]


def _fft_tables(seq):
    n = 2 * seq
    n2 = FFT_N2
    n1 = n // n2
    h1 = n1 // 2
    k1 = jnp.arange(h1, dtype=jnp.int32)
    m1 = ((2 * k1[:, None] + 1) * k1[None, :]) % (2 * n1)
    a1 = m1.astype(F32) * (math.pi / n1)
    f1 = jnp.stack([jnp.cos(a1), -jnp.sin(a1)], axis=1).reshape(n1, h1)
    q = jnp.arange(n2, dtype=jnp.int32)
    kk = 2 * (k1[:, None, None] + n1 * q[None, :, None]) + 1
    a2 = ((kk * q[None, None, :]) % (2 * n)).astype(F32) * (math.pi / n)
    c, s = jnp.cos(a2), jnp.sin(a2)
    t2 = jnp.concatenate([jnp.concatenate([c, s], axis=2), jnp.concatenate([-s, c], axis=2)], axis=1)
    eye = jnp.eye(FFT_ROWS, dtype=F32)
    g1, g1t = jnp.kron(f1, eye), jnp.kron(f1.T, eye)
    return (g1.astype(BF16), g1t.astype(BF16), t2.astype(BF16), jnp.swapaxes(t2, 1, 2).astype(BF16))


def kernel(x, ffn1_pre_g, ffn1_w_gate, ffn1_w_up, ffn1_w_down, ffn1_post_g, mix_pre_g, w_in, hy_conv_w, hy_conv_b, filt_w1, filt_b, filt_freq, filt_w_inner, filt_w_out, hy_skip, sgu_ln_g, sgu_ln_b, sgu_w_s, sgu_b_s, p_a, p_b, w_out, mix_post_g, ffn2_pre_g, ffn2_w_gate, ffn2_w_up, ffn2_w_down, ffn2_post_g):
    bsz, seq, d = x.shape
    depth = w_in.shape[0]
    n_tok = bsz * seq

    z = _filter_positions(seq)
    abs_delta = _decay_rates()
    g1, g1t, t2, t2t = _fft_tables(seq)
    n2 = FFT_N2
    h1 = seq // n2
    n1 = 2 * h1
    nsig = HYENA_ORDER * N_DIRS
    row = lambda v: v[None, :]
    bf = lambda w: w.astype(BF16)

    ffn1 = (ffn1_w_gate, ffn1_w_up, ffn1_w_down)
    ffn2 = (ffn2_w_gate, ffn2_w_up, ffn2_w_down)
    w_in, p_a, p_b, w_out, sgu_w_s = bf(w_in), bf(p_a), bf(p_b), bf(w_out), bf(sgu_w_s)

    xs = x.reshape(n_tok, d)
    for i in range(depth):
        xs = _ffn(xs, i, row(ffn1_pre_g[i]), *ffn1, row(ffn1_post_g[i]))
        pre_g = row(mix_pre_g[i])

        u = _proj_conv(xs, i, pre_g, w_in, hy_conv_w[i], row(hy_conv_b[i]), seq)
        u = u.reshape(HYENA_ORDER + 1, bsz, h1, n2, D_HYENA)
        w1 = jnp.pad(filt_w1[i], ((0, LANES - FILTER_EMB), (0, 0)))
        taps_a = _filter_stage1(z.reshape(h1, n2, LANES), w1, filt_b[i], filt_freq[i], filt_w_inner[i],
                                filt_w_out[i], abs_delta, g1)
        k_spec = _filter_spectrum(t2, taps_a.reshape(nsig, h1, 2 * n2, D_HYENA), seq)
        y, y_part = u, 0
        a1 = _stage1(g1, y, y_part)
        for o in range(HYENA_ORDER):
            last = o == HYENA_ORDER - 1
            b2 = _spectral(t2, t2t, a1.reshape(bsz, h1, 2 * n2, D_HYENA), k_spec, o)
            outs = _inverse_gate(g1t, b2.reshape(bsz, n1, n2, D_HYENA), u, o + 1, y, y_part,
                                 row(hy_skip[i, o]), g1=None if last else g1)
            y, y_part = outs[0], 0
            a1 = None if last else outs[1]
        a = y.reshape(n_tok, D_HYENA)

        b_s = jnp.broadcast_to(sgu_b_s[i][:, :, None], (SGU_GROUPS, SGU_CHUNK, D_SGU // SGU_GROUPS))
        mb = _sgu(xs, i, pre_g, w_in, row(sgu_ln_g[i]), row(sgu_ln_b[i]), sgu_w_s, b_s, p_b)

        xs = _merge(xs, i, pre_g, w_in, a, p_a, mb, w_out, row(mix_post_g[i]))
        xs = _ffn(xs, i, row(ffn2_pre_g[i]), *ffn2, row(ffn2_post_g[i]))
    return xs.reshape(bsz, seq, d)
```
